```python
import math
import jax, jax.numpy as jnp
from jax import lax
import numpy as np

D_MODEL = 2048
BATCH = 4
SEQ = 2048
DEPTH = 1
DEC_BATCH = 32
DEC_SEQ = 1
PAST_LEN = 16384
PAGE_SIZE = 128

FOX_HEADS = 8
FOX_HEAD_DIM = 128
FOX_WIDTH = FOX_HEADS * FOX_HEAD_DIM
RET_HEADS = 4
RET_HEAD_DIM = 256
RET_WIDTH = RET_HEADS * RET_HEAD_DIM
N_BRANCHES = 2
Q_BLOCK = 128
RET_CHUNK = 128
ROPE_BASE = 10000.0
EPS = 1e-6
NEG_INF = -1e30
IN_SIZES = [FOX_WIDTH] * 4 + [FOX_HEADS] + [RET_WIDTH] * 4 + [D_MODEL] * N_BRANCHES
N_IN = sum(IN_SIZES)

kernel_name = "fox_retention_gated_hybrid_step"

F32 = jnp.float32


def rmsnorm(x, w):
    x32 = x.astype(F32)
    y = x32 * lax.rsqrt(jnp.mean(x32 * x32, axis=-1, keepdims=True) + EPS) * w.astype(F32)
    return y.astype(x.dtype)


def split_in(p):
    idx = np.cumsum(IN_SIZES)[:-1].tolist()
    return jnp.split(p, idx, axis=-1)


def rotary(x, pos):
    half = RET_HEAD_DIM // 2
    inv = 1.0 / (ROPE_BASE ** (jnp.arange(half, dtype=F32) / half))
    ang = pos.astype(F32)[:, None] * inv[None, :]
    cos = jnp.cos(ang)[None, :, None, :]
    sin = jnp.sin(ang)[None, :, None, :]
    x1 = x[..., :half].astype(F32)
    x2 = x[..., half:].astype(F32)
    return jnp.concatenate([x1 * cos - x2 * sin, x2 * cos + x1 * sin], axis=-1).astype(x.dtype)


def ret_log_gamma():
    return jnp.log(1.0 - 2.0 ** (-5.0 - jnp.arange(RET_HEADS, dtype=F32)))


def retention_chunk(state, q, k, v, log_gamma):
    L = q.shape[2]
    i = jnp.arange(L, dtype=F32)
    diff = i[:, None] - i[None, :]
    lg = log_gamma[:, None, None]
    decay = jnp.where(diff >= 0, jnp.exp(lg * jnp.maximum(diff, 0.0)), 0.0)
    qf, kf, vf = q.astype(F32), k.astype(F32), v.astype(F32)
    scores = jnp.einsum('bhld,bhmd->bhlm', qf, kf) * decay[None]
    cross = jnp.exp(log_gamma[:, None] * (i + 1.0))[None, :, :, None]
    o = jnp.einsum('bhlm,bhme->bhle', scores, vf) + cross * jnp.einsum('bhld,bhde->bhle', qf, state)
    k_dec = kf * jnp.exp(log_gamma[:, None] * (L - 1.0 - i))[None, :, :, None]
    new_state = jnp.exp(log_gamma * L)[None, :, None, None] * state + jnp.einsum('bhld,bhle->bhde', k_dec, vf)
    return o, new_state


def retention_prompt(q, k, v):
    B, S, H, _ = q.shape
    nc = S // RET_CHUNK
    log_gamma = ret_log_gamma()

    def to_chunks(t):
        return t.reshape(B, nc, RET_CHUNK, H, t.shape[-1]).transpose(1, 0, 3, 2, 4)

    def step(state, qkv):
        o, st = retention_chunk(state, qkv[0], qkv[1], qkv[2], log_gamma)
        return st, o

    init = jnp.zeros((B, H, RET_HEAD_DIM, RET_HEAD_DIM), F32)
    final, o = lax.scan(step, init, (to_chunks(q), to_chunks(k), to_chunks(v)))
    o = o.transpose(1, 0, 3, 2, 4).reshape(B, S, H, RET_HEAD_DIM)
    return o, final


def retention_sample(state, q, k, v):
    t = lambda a: a.transpose(0, 2, 1, 3)
    o, st = retention_chunk(state.astype(F32), t(q), t(k), t(v), ret_log_gamma())
    return t(o), st


def fox_prompt(q, k, v, logf):
    B, S, H, D = q.shape
    scale = FOX_HEAD_DIM ** -0.5
    c = jnp.cumsum(logf, axis=1)
    c_t = c.transpose(0, 2, 1)
    nb = S // Q_BLOCK
    qb = q.reshape(B, nb, Q_BLOCK, H, D).transpose(1, 0, 2, 3, 4)
    cb = c_t.reshape(B, H, nb, Q_BLOCK).transpose(2, 0, 1, 3)
    kpos = jnp.arange(S)

    def one_block(args):
        qi, ci, start = args
        s = jnp.einsum('bqhd,bkhd->bhqk', qi, k).astype(F32) * scale
        s = s + ci[..., None] - c_t[:, :, None, :]
        qpos = start + jnp.arange(Q_BLOCK)
        mask = kpos[None, :] <= qpos[:, None]
        p = jax.nn.softmax(jnp.where(mask[None, None], s, NEG_INF), axis=-1)
        return jnp.einsum('bhqk,bkhd->bqhd', p.astype(v.dtype), v)

    out = lax.map(one_block, (qb, cb, jnp.arange(nb) * Q_BLOCK))
    return out.transpose(1, 0, 2, 3, 4).reshape(B, S, H * D)


def fox_sample(q, k, v, logf, cache_k, cache_v, cache_logf, page_table):
    B, L, H, D = q.shape
    scale = FOX_HEAD_DIM ** -0.5
    past = page_table.shape[1] * PAGE_SIZE
    k_past = cache_k[page_table].reshape(B, past, H, D)
    v_past = cache_v[page_table].reshape(B, past, H, D)
    lf_past = cache_logf[page_table].reshape(B, past, H).astype(F32)
    r = lax.cumsum(lf_past, axis=1, reverse=True) - lf_past
    cn = jnp.cumsum(logf, axis=1)
    cn_t = cn.transpose(0, 2, 1)
    s_past = jnp.einsum('blhd,bphd->bhlp', q, k_past).astype(F32) * scale
    s_past = s_past + cn_t[..., None] + r.transpose(0, 2, 1)[:, :, None, :]
    s_new = jnp.einsum('blhd,bmhd->bhlm', q, k).astype(F32) * scale
    s_new = s_new + cn_t[..., None] - cn_t[:, :, None, :]
    pos = jnp.arange(L)
    s_new = jnp.where((pos[None, :] <= pos[:, None])[None, None], s_new, NEG_INF)
    p = jax.nn.softmax(jnp.concatenate([s_past, s_new], axis=-1), axis=-1).astype(v.dtype)
    out = jnp.einsum('bhlp,bphd->blhd', p[..., :past], v_past) + jnp.einsum('bhlm,bmhd->blhd', p[..., past:], v)
    return out.reshape(B, L, H * D)


def project_in(x, norm_w, w_in, b_f, pos):
    B, L, _ = x.shape
    h = rmsnorm(x, norm_w)
    qa, ka, va, za, fa, qb, kb, vb, zb, ga, gb = split_in(h @ w_in)
    shp_a = (B, L, FOX_HEADS, FOX_HEAD_DIM)
    shp_b = (B, L, RET_HEADS, RET_HEAD_DIM)
    logf = jax.nn.log_sigmoid(fa.astype(F32) + b_f.astype(F32))
    qb = rotary(qb.reshape(shp_b), pos)
    kb = rotary(kb.reshape(shp_b), pos) * (RET_HEAD_DIM ** -0.5)
    return (qa.reshape(shp_a), ka.reshape(shp_a), va.reshape(shp_a), za, logf,
            qb, kb, vb.reshape(shp_b), zb, ga, gb)


def merge_out(x, o_a, z_a, o_b, z_b, g_a, g_b, ret_norm_w, w_up_a, w_up_b, w_o):
    B, L, _ = x.shape
    o_b = o_b * lax.rsqrt(jnp.mean(o_b * o_b, axis=-1, keepdims=True) + EPS)
    o_b = (o_b.reshape(B, L, RET_WIDTH) * ret_norm_w.astype(F32)).astype(x.dtype)
    u_a = (o_a * jax.nn.silu(z_a)) @ w_up_a
    u_b = (o_b * jax.nn.silu(z_b)) @ w_up_b
    merged = jax.nn.sigmoid(g_a) * u_a + jax.nn.sigmoid(g_b) * u_b
    return x + merged @ w_o


def setup_inputs(seed: int = 0) -> dict:
    key = jax.random.key(seed)
    ks = jax.random.split(key, 20)
    n_pages = PAST_LEN // PAGE_SIZE
    n_used = DEC_BATCH * n_pages
    n_pool = n_used + max(1, n_used // 4)
    nrm = jax.random.normal
    x_prompt = nrm(ks[0], (BATCH, SEQ, D_MODEL), F32)
    x_sample = nrm(ks[1], (DEC_BATCH, DEC_SEQ, D_MODEL), F32)
    cache_k = nrm(ks[2], (DEPTH, n_pool, PAGE_SIZE, FOX_HEADS, FOX_HEAD_DIM), F32)
    cache_v = nrm(ks[3], (DEPTH, n_pool, PAGE_SIZE, FOX_HEADS, FOX_HEAD_DIM), F32)
    b_f = jnp.linspace(1.0, 6.0, FOX_HEADS, dtype=F32)[None, :] + 0.1 * nrm(ks[4], (DEPTH, FOX_HEADS), F32)
    cache_logf = jax.nn.log_sigmoid(b_f[:, None, None, :] + 0.5 * nrm(ks[5], (DEPTH, n_pool, PAGE_SIZE, FOX_HEADS), F32))
    state_ret = nrm(ks[6], (DEPTH, DEC_BATCH, RET_HEADS, RET_HEAD_DIM, RET_HEAD_DIM), F32)
    page_table = jax.random.permutation(ks[7], n_pool)[:n_used].reshape(DEC_BATCH, n_pages).astype(jnp.int32)
    norm_w = 1.0 + 0.02 * nrm(ks[8], (DEPTH, D_MODEL), F32)
    w_in = nrm(ks[9], (DEPTH, D_MODEL, N_IN), F32) * D_MODEL ** -0.5
    ret_norm_w = 1.0 + 0.02 * nrm(ks[10], (DEPTH, RET_WIDTH), F32)
    w_up_a = nrm(ks[11], (DEPTH, FOX_WIDTH, D_MODEL), F32) * FOX_WIDTH ** -0.5
    w_up_b = nrm(ks[12], (DEPTH, RET_WIDTH, D_MODEL), F32) * RET_WIDTH ** -0.5
    w_o = nrm(ks[13], (DEPTH, D_MODEL, D_MODEL), F32) * D_MODEL ** -0.5
    final_norm_w = 1.0 + 0.02 * nrm(ks[14], (D_MODEL,), F32)
    return {"x_prompt": x_prompt, "x_sample": x_sample, "cache_k": cache_k, "cache_v": cache_v,
            "cache_logf": cache_logf, "state_ret": state_ret, "page_table": page_table,
            "norm_w": norm_w, "w_in": w_in, "b_f": b_f, "ret_norm_w": ret_norm_w,
            "w_up_a": w_up_a, "w_up_b": w_up_b, "w_o": w_o, "final_norm_w": final_norm_w}


def reference(x_prompt, x_sample, cache_k, cache_v, cache_logf, state_ret, page_table,
              norm_w, w_in, b_f, ret_norm_w, w_up_a, w_up_b, w_o, final_norm_w):
    S = x_prompt.shape[1]
    L = x_sample.shape[1]
    past = page_table.shape[1] * PAGE_SIZE
    pos_p = jnp.arange(S, dtype=jnp.int32)
    pos_s = past + jnp.arange(L, dtype=jnp.int32)
    xp, xs = x_prompt, x_sample
    kp_l, vp_l, fp_l, rp_l, ks_l, vs_l, fs_l, rs_l = [], [], [], [], [], [], [], []
    for layer in range(DEPTH):
        qa, ka, va, za, lf, qb, kb, vb, zb, ga, gb = project_in(xp, norm_w[layer], w_in[layer], b_f[layer], pos_p)
        o_a = fox_prompt(qa, ka, va, lf)
        o_b, st_p = retention_prompt(qb, kb, vb)
        xp = merge_out(xp, o_a, za, o_b, zb, ga, gb, ret_norm_w[layer], w_up_a[layer], w_up_b[layer], w_o[layer])
        kp_l.append(ka); vp_l.append(va); fp_l.append(lf); rp_l.append(st_p)
        qa, ka, va, za, lf, qb, kb, vb, zb, ga, gb = project_in(xs, norm_w[layer], w_in[layer], b_f[layer], pos_s)
        o_a = fox_sample(qa, ka, va, lf, cache_k[layer], cache_v[layer], cache_logf[layer], page_table)
        o_b, st_s = retention_sample(state_ret[layer], qb, kb, vb)
        xs = merge_out(xs, o_a, za, o_b, zb, ga, gb, ret_norm_w[layer], w_up_a[layer], w_up_b[layer], w_o[layer])
        ks_l.append(ka); vs_l.append(va); fs_l.append(lf); rs_l.append(st_s)
    y_prompt = rmsnorm(xp, final_norm_w)
    y_sample = rmsnorm(xs, final_norm_w)
    return (y_prompt, y_sample, jnp.stack(kp_l), jnp.stack(vp_l), jnp.stack(fp_l), jnp.stack(rp_l),
            jnp.stack(ks_l), jnp.stack(vs_l), jnp.stack(fs_l), jnp.stack(rs_l))
```

```python
import functools

import jax
import jax.numpy as jnp
from jax import lax
from jax.experimental import pallas as pl
from jax.experimental.pallas import tpu as pltpu

F32 = jnp.float32
BF16 = jnp.bfloat16

D_MODEL = 2048
FOX_HEADS = 8
FOX_HEAD_DIM = 128
FOX_WIDTH = FOX_HEADS * FOX_HEAD_DIM
RET_HEADS = 4
RET_HEAD_DIM = 256
RET_WIDTH = RET_HEADS * RET_HEAD_DIM
RET_CHUNK = 128
PAGE_SIZE = 128
ROPE_BASE = 10000.0
EPS = 1e-6
NEG_INF = -1e30
FOX_SCALE = FOX_HEAD_DIM ** -0.5
RET_K_SCALE = RET_HEAD_DIM ** -0.5

LANES = 128
VMEM_LIMIT = 56 * 1024 * 1024

REST_QA, REST_ZA, REST_QB, REST_KB, REST_VB, REST_ZB, REST_GA, REST_GB = (
    0, 1024, 2048, 3072, 4096, 5120, 6144, 8192)
REST_WIDTH = 10240
PROJ_TN = 512
N_KV_TILES = FOX_WIDTH // PROJ_TN

NT_DIMS = (((1,), (1,)), ((), ()))
TN_DIMS = (((0,), (0,)), ((), ()))


def _cparams(sem, vmem=VMEM_LIMIT):
    return pltpu.CompilerParams(dimension_semantics=sem, vmem_limit_bytes=vmem)


def _inproj_kernel(x_ref, nw_ref, w_ref, wf_ref, k_ref, v_ref, rest_ref, f_ref, h_ref):
    j = pl.program_id(1)

    @pl.when(j == 0)
    def _():
        x = x_ref[...]
        ms = jnp.mean(x * x, axis=-1, keepdims=True)
        h = (x * lax.rsqrt(ms + EPS) * nw_ref[...]).astype(BF16)
        h_ref[...] = h
        f_ref[...] = jnp.dot(h, wf_ref[...], preferred_element_type=F32)

    acc = jnp.dot(h_ref[...], w_ref[...], preferred_element_type=F32)

    @pl.when(j < N_KV_TILES)
    def _():
        k_ref[...] = acc

    @pl.when((j >= N_KV_TILES) & (j < 2 * N_KV_TILES))
    def _():
        v_ref[...] = acc

    @pl.when(j >= 2 * N_KV_TILES)
    def _():
        rest_ref[...] = acc


def _inproj(x, norm_w, w_r, w_f, tm):
    m = x.shape[0]
    n_tiles = w_r.shape[1] // PROJ_TN
    kv_last = N_KV_TILES - 1
    return pl.pallas_call(
        _inproj_kernel,
        grid=(m // tm, n_tiles),
        in_specs=[
            pl.BlockSpec((tm, D_MODEL), lambda i, j: (i, 0)),
            pl.BlockSpec((1, D_MODEL), lambda i, j: (0, 0)),
            pl.BlockSpec((D_MODEL, PROJ_TN), lambda i, j: (0, j)),
            pl.BlockSpec((D_MODEL, LANES), lambda i, j: (0, 0)),
        ],
        out_specs=[
            pl.BlockSpec((tm, PROJ_TN), lambda i, j: (i, jnp.minimum(j, kv_last))),
            pl.BlockSpec((tm, PROJ_TN),
                         lambda i, j: (i, jnp.clip(j - N_KV_TILES, 0, kv_last))),
            pl.BlockSpec((tm, PROJ_TN),
                         lambda i, j: (i, jnp.maximum(j - 2 * N_KV_TILES, 0))),
            pl.BlockSpec((tm, LANES), lambda i, j: (i, 0)),
        ],
        out_shape=[
            jax.ShapeDtypeStruct((m, FOX_WIDTH), F32),
            jax.ShapeDtypeStruct((m, FOX_WIDTH), F32),
            jax.ShapeDtypeStruct((m, REST_WIDTH), F32),
            jax.ShapeDtypeStruct((m, LANES), F32),
        ],
        scratch_shapes=[pltpu.VMEM((tm, D_MODEL), BF16)],
        compiler_params=_cparams(("arbitrary", "arbitrary")),
        name="inproj",
    )(x, norm_w, w_r, w_f)


def _logf_kernel(f_ref, bf_ref, logf_ref):
    lf = jax.nn.log_sigmoid(f_ref[...] + bf_ref[...])
    logf_ref[...] = lf[:, :FOX_HEADS]


def _logf(f, bf_pad):
    m = f.shape[0]
    return pl.pallas_call(
        _logf_kernel,
        grid=(1,),
        in_specs=[pl.BlockSpec((m, LANES), lambda i: (0, 0)),
                  pl.BlockSpec((1, LANES), lambda i: (0, 0))],
        out_specs=pl.BlockSpec((m, FOX_HEADS), lambda i: (0, 0)),
        out_shape=jax.ShapeDtypeStruct((m, FOX_HEADS), F32),
        name="logf_sample",
    )(f, bf_pad)


def _logf_cumsum_kernel(f_ref, bf_ref, logf_ref, c_ref):
    seq = f_ref.shape[0]
    lf = jax.nn.log_sigmoid(f_ref[...] + bf_ref[...])
    logf_ref[...] = lf[:, :FOX_HEADS]
    lft = lf.T
    row = lax.broadcasted_iota(jnp.int32, (LANES, LANES), 0)
    col = lax.broadcasted_iota(jnp.int32, (LANES, LANES), 1)
    upper = (row <= col).astype(F32)
    carry = jnp.zeros((FOX_HEADS, 1), F32)
    for blk in range(seq // LANES):
        seg = lft[:FOX_HEADS, blk * LANES:(blk + 1) * LANES]
        cs = jnp.dot(seg, upper, preferred_element_type=F32,
                     precision=lax.Precision.HIGHEST) + carry
        c_ref[0, :, blk * LANES:(blk + 1) * LANES] = cs
        carry = cs[:, LANES - 1:LANES]


def _logf_cumsum(f, bf_pad, batch, seq):
    return pl.pallas_call(
        _logf_cumsum_kernel,
        grid=(batch,),
        in_specs=[pl.BlockSpec((seq, LANES), lambda b: (b, 0)),
                  pl.BlockSpec((1, LANES), lambda b: (0, 0))],
        out_specs=[pl.BlockSpec((seq, FOX_HEADS), lambda b: (b, 0)),
                   pl.BlockSpec((1, FOX_HEADS, seq), lambda b: (b, 0, 0))],
        out_shape=[jax.ShapeDtypeStruct((batch * seq, FOX_HEADS), F32),
                   jax.ShapeDtypeStruct((batch, FOX_HEADS, seq), F32)],
        compiler_params=_cparams(("arbitrary",)),
        name="logf_cumsum",
    )(f, bf_pad)


FOX_TQ = 256
FOX_TK = 256


def _fox_prompt_kernel(q_ref, k_ref, v_ref, c_ref, o_ref, kb_ref, vb_ref):
    h = pl.program_id(1)
    qi = pl.program_id(2)

    @pl.when(qi == 0)
    def _():
        kb_ref[...] = k_ref[...].astype(BF16)
        vb_ref[...] = v_ref[...].astype(BF16)

    q = q_ref[...].astype(BF16)

    def block(kbi, carry, masked):
        m, l, acc = carry
        start = pl.multiple_of(kbi * FOX_TK, FOX_TK)
        k = kb_ref[pl.ds(start, FOX_TK), :]
        v = vb_ref[pl.ds(start, FOX_TK), :]
        s = lax.dot_general(q, k, NT_DIMS, preferred_element_type=F32) * FOX_SCALE
        s = s - c_ref[0, pl.ds(h, 1), pl.ds(start, FOX_TK)]
        if masked:
            row = lax.broadcasted_iota(jnp.int32, (FOX_TQ, FOX_TK), 0)
            col = lax.broadcasted_iota(jnp.int32, (FOX_TQ, FOX_TK), 1)
            s = jnp.where(col <= row, s, NEG_INF)
        m_new = jnp.maximum(m, jnp.max(s, axis=-1, keepdims=True))
        alpha = jnp.exp(m - m_new)
        p = jnp.exp(s - m_new)
        l = alpha * l + jnp.sum(p, axis=-1, keepdims=True)
        acc = alpha * acc + jnp.dot(p.astype(BF16), v, preferred_element_type=F32)
        return m_new, l, acc

    init = (jnp.full((FOX_TQ, 1), NEG_INF, F32), jnp.zeros((FOX_TQ, 1), F32),
            jnp.zeros((FOX_TQ, FOX_HEAD_DIM), F32))
    carry = lax.fori_loop(0, qi, lambda kbi, c: block(kbi, c, False), init)
    _, l, acc = block(qi, carry, True)
    o_ref[...] = acc / l


def _fox_prompt(rest, k, v, c_t, batch, seq):
    nq = seq // FOX_TQ
    return pl.pallas_call(
        _fox_prompt_kernel,
        grid=(batch, FOX_HEADS, nq),
        in_specs=[
            pl.BlockSpec((FOX_TQ, FOX_HEAD_DIM),
                         lambda b, h, qi: (b * nq + qi, REST_QA // FOX_HEAD_DIM + h)),
            pl.BlockSpec((seq, FOX_HEAD_DIM), lambda b, h, qi: (b, h)),
            pl.BlockSpec((seq, FOX_HEAD_DIM), lambda b, h, qi: (b, h)),
            pl.BlockSpec((1, FOX_HEADS, seq), lambda b, h, qi: (b, 0, 0)),
        ],
        out_specs=pl.BlockSpec((FOX_TQ, FOX_HEAD_DIM), lambda b, h, qi: (b * nq + qi, h)),
        out_shape=jax.ShapeDtypeStruct((batch * seq, FOX_WIDTH), F32),
        scratch_shapes=[pltpu.VMEM((seq, FOX_HEAD_DIM), BF16),
                        pltpu.VMEM((seq, FOX_HEAD_DIM), BF16)],
        compiler_params=_cparams(("arbitrary", "arbitrary", "arbitrary")),
        name="fox_prompt",
    )(rest, k, v, c_t)


def _rotary(x, cos, sin):
    half = RET_HEAD_DIM // 2
    x1 = x[:, :half]
    x2 = x[:, half:]
    return jnp.concatenate([x1 * cos - x2 * sin, x2 * cos + x1 * sin], axis=-1)


def _head_rmsnorm(o, w):
    return o * lax.rsqrt(jnp.mean(o * o, axis=-1, keepdims=True) + EPS) * w


def _ret_prompt_kernel(lg_ref, q_ref, k_ref, v_ref, cos_ref, sin_ref, w_ref,
                       o_ref, st_ref):
    h = pl.program_id(1)
    lg = lg_ref[h]
    n_chunks = q_ref.shape[0] // RET_CHUNK
    ri = lax.broadcasted_iota(jnp.int32, (RET_CHUNK, RET_CHUNK), 0)
    ci = lax.broadcasted_iota(jnp.int32, (RET_CHUNK, RET_CHUNK), 1)
    diff = (ri - ci).astype(F32)
    decay = jnp.where(diff >= 0, jnp.exp(lg * jnp.maximum(diff, 0.0)), 0.0)
    pos = lax.broadcasted_iota(jnp.int32, (RET_CHUNK, RET_HEAD_DIM), 0).astype(F32)
    cross = jnp.exp(lg * (pos + 1.0))
    kdec = jnp.exp(lg * (RET_CHUNK - 1.0 - pos))
    g_chunk = jnp.exp(jnp.full((1, RET_HEAD_DIM), lg * RET_CHUNK, F32))
    w = w_ref[...]

    st_ref[0, 0] = jnp.zeros((RET_HEAD_DIM, RET_HEAD_DIM), F32)

    def chunk(c, _):
        r0 = pl.multiple_of(c * RET_CHUNK, RET_CHUNK)
        rows = pl.ds(r0, RET_CHUNK)
        cos = cos_ref[rows, :]
        sin = sin_ref[rows, :]
        qr = _rotary(q_ref[rows, :], cos, sin)
        kr = _rotary(k_ref[rows, :], cos, sin) * RET_K_SCALE
        qb = qr.astype(BF16)
        kb = kr.astype(BF16)
        vb = v_ref[rows, :].astype(BF16)
        st = st_ref[0, 0]
        scores = lax.dot_general(qb, kb, NT_DIMS, preferred_element_type=F32) * decay
        o = jnp.dot(scores.astype(BF16), vb, preferred_element_type=F32)
        o = o + cross * jnp.dot(qb, st.astype(BF16), preferred_element_type=F32)
        kd = (kr * kdec).astype(BF16)
        st_ref[0, 0] = g_chunk * st + lax.dot_general(kd, vb, TN_DIMS,
                                                      preferred_element_type=F32)
        o_ref[rows, :] = _head_rmsnorm(o, w)
        return 0

    lax.fori_loop(0, n_chunks, chunk, 0)


def _ret_prompt(log_gamma, rest, cos, sin, ret_norm_w, batch, seq):
    hd = RET_HEAD_DIM
    return pl.pallas_call(
        _ret_prompt_kernel,
        grid=(batch, RET_HEADS),
        in_specs=[
            pl.BlockSpec(memory_space=pltpu.SMEM),
            pl.BlockSpec((seq, hd), lambda b, h: (b, REST_QB // hd + h)),
            pl.BlockSpec((seq, hd), lambda b, h: (b, REST_KB // hd + h)),
            pl.BlockSpec((seq, hd), lambda b, h: (b, REST_VB // hd + h)),
            pl.BlockSpec((seq, hd // 2), lambda b, h: (0, 0)),
            pl.BlockSpec((seq, hd // 2), lambda b, h: (0, 0)),
            pl.BlockSpec((1, hd), lambda b, h: (0, h)),
        ],
        out_specs=[
            pl.BlockSpec((seq, hd), lambda b, h: (b, h)),
            pl.BlockSpec((1, 1, hd, hd), lambda b, h: (b, h, 0, 0)),
        ],
        out_shape=[
            jax.ShapeDtypeStruct((batch * seq, RET_WIDTH), F32),
            jax.ShapeDtypeStruct((batch, RET_HEADS, hd, hd), F32),
        ],
        compiler_params=_cparams(("arbitrary", "arbitrary")),
        name="ret_prompt",
    )(log_gamma, rest, rest, rest, cos, sin, ret_norm_w)


def _ret_sample_kernel(lg_ref, q_ref, k_ref, v_ref, cos_ref, sin_ref, w_ref, st_ref,
                       o_ref, nst_ref):
    hd = RET_HEAD_DIM
    cos = cos_ref[...]
    sin = sin_ref[...]
    first_row = lax.broadcasted_iota(jnp.int32, (16, hd), 0) == 0
    for h in range(RET_HEADS):
        cols = slice(h * hd, (h + 1) * hd)
        gamma = jnp.exp(jnp.full((1, hd), lg_ref[h], F32))
        qr = _rotary(q_ref[:, cols], cos, sin)
        kr = _rotary(k_ref[:, cols], cos, sin) * RET_K_SCALE
        vh = v_ref[:, cols]
        st = st_ref[0, h]
        q16 = jnp.broadcast_to(qr, (16, hd)).astype(BF16)
        qs = jnp.dot(q16, st.astype(BF16), preferred_element_type=F32)[0:1]
        score = jnp.sum(qr * kr, axis=-1, keepdims=True)
        o = score * vh + gamma * qs
        k16 = jnp.where(first_row, jnp.broadcast_to(kr, (16, hd)), 0.0).astype(BF16)
        v16 = jnp.broadcast_to(vh, (16, hd)).astype(BF16)
        nst_ref[0, h] = gamma * st + lax.dot_general(k16, v16, TN_DIMS,
                                                     preferred_element_type=F32)
        o_ref[:, cols] = _head_rmsnorm(o, w_ref[:, cols])


def _ret_sample(log_gamma, rest3, cos, sin, ret_norm_w, state):
    nb = rest3.shape[0]
    hd = RET_HEAD_DIM
    return pl.pallas_call(
        _ret_sample_kernel,
        grid=(nb,),
        in_specs=[
            pl.BlockSpec(memory_space=pltpu.SMEM),
            pl.BlockSpec((None, 1, RET_WIDTH), lambda b: (b, 0, REST_QB // RET_WIDTH)),
            pl.BlockSpec((None, 1, RET_WIDTH), lambda b: (b, 0, REST_KB // RET_WIDTH)),
            pl.BlockSpec((None, 1, RET_WIDTH), lambda b: (b, 0, REST_VB // RET_WIDTH)),
            pl.BlockSpec((1, hd // 2), lambda b: (0, 0)),
            pl.BlockSpec((1, hd // 2), lambda b: (0, 0)),
            pl.BlockSpec((1, RET_WIDTH), lambda b: (0, 0)),
            pl.BlockSpec((1, RET_HEADS, hd, hd), lambda b: (b, 0, 0, 0)),
        ],
        out_specs=[
            pl.BlockSpec((None, 1, RET_WIDTH), lambda b: (b, 0, 0)),
            pl.BlockSpec((1, RET_HEADS, hd, hd), lambda b: (b, 0, 0, 0)),
        ],
        out_shape=[
            jax.ShapeDtypeStruct((nb, 1, RET_WIDTH), F32),
            jax.ShapeDtypeStruct((nb, RET_HEADS, hd, hd), F32),
        ],
        compiler_params=_cparams(("arbitrary",)),
        name="ret_sample",
    )(log_gamma, rest3, rest3, rest3, cos, sin, ret_norm_w, state)


DEC_PAGES = 8


def _lane_block_broadcast(x):
    return jnp.concatenate(
        [jnp.broadcast_to(x[:, h:h + 1], (1, FOX_HEAD_DIM)) for h in range(FOX_HEADS)],
        axis=1)


def _fox_decode_kernel(pt_ref, q_ref, kn_ref, vn_ref, lfn_ref, *refs):
    del pt_ref
    np_ = DEC_PAGES
    k_refs = refs[:np_]
    v_refs = refs[np_:2 * np_]
    lf_refs = refs[2 * np_:3 * np_]
    o_ref = refs[3 * np_]
    m_ref, l_ref, acc_ref, carry_ref, qt_ref, e_ref, pad_ref, cn_ref = refs[3 * np_ + 1:]
    c = pl.program_id(1)
    n_chunks = pl.num_programs(1)

    @pl.when(c == 0)
    def _():
        m_ref[...] = jnp.full((1, LANES), NEG_INF, F32)
        l_ref[...] = jnp.zeros((1, LANES), F32)
        acc_ref[...] = jnp.zeros((8, FOX_WIDTH), F32)
        carry_ref[...] = jnp.zeros((1, LANES), F32)
        head = lax.broadcasted_iota(jnp.int32, (LANES, FOX_WIDTH), 0)
        blk = lax.broadcasted_iota(jnp.int32, (LANES, FOX_WIDTH), 1) // FOX_HEAD_DIM
        on_diag = head == blk
        e_ref[...] = jnp.where(on_diag, 1.0, 0.0).astype(BF16)
        qt_ref[...] = jnp.where(on_diag, jnp.broadcast_to(q_ref[...], (LANES, FOX_WIDTH)),
                                0.0).astype(BF16)
        pad_ref[...] = jnp.zeros((PAGE_SIZE, LANES), F32)
        cn_ref[...] = jnp.zeros((8, LANES), F32)
        cn_ref[0:1, 0:FOX_HEADS] = lfn_ref[...]

    qt = qt_ref[...]
    e = e_ref[...]
    cn = cn_ref[0:1, :]
    row = lax.broadcasted_iota(jnp.int32, (PAGE_SIZE, PAGE_SIZE), 0)
    col = lax.broadcasted_iota(jnp.int32, (PAGE_SIZE, PAGE_SIZE), 1)
    after = (col > row).astype(F32)

    carry = carry_ref[...]
    s_pages = [None] * np_
    for j in reversed(range(np_)):
        pad_ref[:, 0:FOX_HEADS] = lf_refs[j][...]
        lf = pad_ref[...]
        r = jnp.dot(after, lf, preferred_element_type=F32,
                    precision=lax.Precision.HIGHEST)
        s = lax.dot_general(k_refs[j][...].astype(BF16), qt, NT_DIMS,
                            preferred_element_type=F32) * FOX_SCALE
        s_pages[j] = s + (r + (carry + cn))
        carry = carry + (r[0:1, :] + lf[0:1, :])
    carry_ref[...] = carry

    m_old = m_ref[...]
    m_new = m_old
    for j in range(np_):
        m_new = jnp.maximum(m_new, jnp.max(s_pages[j], axis=0, keepdims=True))
    alpha = jnp.exp(m_old - m_new)
    l = alpha * l_ref[...]
    contrib = jnp.zeros((8, FOX_WIDTH), F32)
    for j in range(np_):
        p = jnp.exp(s_pages[j] - m_new)
        l = l + jnp.sum(p, axis=0, keepdims=True)
        pb = jnp.dot(p.astype(BF16), e, preferred_element_type=F32)
        pv = pb * v_refs[j][...]
        contrib = contrib + jnp.sum(pv.reshape(PAGE_SIZE // 8, 8, FOX_WIDTH), axis=0)
    acc = _lane_block_broadcast(alpha) * acc_ref[...] + contrib
    m_ref[...] = m_new
    l_ref[...] = l
    acc_ref[...] = acc

    @pl.when(c == n_chunks - 1)
    def _():
        kn = jnp.broadcast_to(kn_ref[...], (16, FOX_WIDTH)).astype(BF16)
        sn = lax.dot_general(kn, qt, NT_DIMS, preferred_element_type=F32)[0:1] * FOX_SCALE
        m_fin = jnp.maximum(m_new, sn)
        a_fin = jnp.exp(m_new - m_fin)
        pn = jnp.exp(sn - m_fin)
        l_fin = a_fin * l + pn
        past = jnp.sum(acc, axis=0, keepdims=True)
        out = (_lane_block_broadcast(a_fin) * past
               + _lane_block_broadcast(pn) * vn_ref[...])
        o_ref[...] = out / _lane_block_broadcast(l_fin)


def _fox_decode(page_table, q3, kn3, vn3, lfn3, cache_k, cache_v, cache_logf):
    nb, n_pages = page_table.shape
    n_chunks = n_pages // DEC_PAGES

    def page_map(j):
        def index_map(b, c, pt):
            return (pt[b, (n_chunks - 1 - c) * DEC_PAGES + j], 0, 0)
        return index_map

    row_spec = pl.BlockSpec((None, 1, FOX_WIDTH), lambda b, c, pt: (b, 0, 0))
    in_specs = [row_spec, row_spec, row_spec,
                pl.BlockSpec((None, 1, FOX_HEADS), lambda b, c, pt: (b, 0, 0))]
    in_specs += [pl.BlockSpec((None, PAGE_SIZE, FOX_WIDTH), page_map(j))
                 for j in range(DEC_PAGES)]
    in_specs += [pl.BlockSpec((None, PAGE_SIZE, FOX_WIDTH), page_map(j))
                 for j in range(DEC_PAGES)]
    in_specs += [pl.BlockSpec((None, PAGE_SIZE, FOX_HEADS), page_map(j))
                 for j in range(DEC_PAGES)]
    grid_spec = pltpu.PrefetchScalarGridSpec(
        num_scalar_prefetch=1,
        grid=(nb, n_chunks),
        in_specs=in_specs,
        out_specs=pl.BlockSpec((None, 1, FOX_WIDTH), lambda b, c, pt: (b, 0, 0)),
        scratch_shapes=[
            pltpu.VMEM((1, LANES), F32),
            pltpu.VMEM((1, LANES), F32),
            pltpu.VMEM((8, FOX_WIDTH), F32),
            pltpu.VMEM((1, LANES), F32),
            pltpu.VMEM((LANES, FOX_WIDTH), BF16),
            pltpu.VMEM((LANES, FOX_WIDTH), BF16),
            pltpu.VMEM((PAGE_SIZE, LANES), F32),
            pltpu.VMEM((8, LANES), F32),
        ],
    )
    return pl.pallas_call(
        _fox_decode_kernel,
        grid_spec=grid_spec,
        out_shape=jax.ShapeDtypeStruct((nb, 1, FOX_WIDTH), F32),
        compiler_params=_cparams(("arbitrary", "arbitrary")),
        name="fox_decode",
    )(page_table, q3, kn3, vn3, lfn3,
      *([cache_k] * DEC_PAGES), *([cache_v] * DEC_PAGES), *([cache_logf] * DEC_PAGES))


def _merge_kernel(x_ref, oa_ref, za_ref, ob_ref, zb_ref, ga_ref, gb_ref,
                  wua_ref, wub_ref, wo_ref, fnw_ref, y_ref):
    a = (oa_ref[...] * jax.nn.silu(za_ref[...])).astype(BF16)
    b = (ob_ref[...] * jax.nn.silu(zb_ref[...])).astype(BF16)
    ua = jnp.dot(a, wua_ref[...], preferred_element_type=F32)
    ub = jnp.dot(b, wub_ref[...], preferred_element_type=F32)
    merged = jax.nn.sigmoid(ga_ref[...]) * ua + jax.nn.sigmoid(gb_ref[...]) * ub
    out = x_ref[...] + jnp.dot(merged.astype(BF16), wo_ref[...], preferred_element_type=F32)
    ms = jnp.mean(out * out, axis=-1, keepdims=True)
    y_ref[...] = out * lax.rsqrt(ms + EPS) * fnw_ref[...]


def _merge(x, o_a, o_b, rest, w_up_a, w_up_b, w_o, final_norm_w, tm):
    m = x.shape[0]
    const = dict(pipeline_mode=pl.Buffered(1))
    return pl.pallas_call(
        _merge_kernel,
        grid=(m // tm,),
        in_specs=[
            pl.BlockSpec((tm, D_MODEL), lambda i: (i, 0)),
            pl.BlockSpec((tm, FOX_WIDTH), lambda i: (i, 0)),
            pl.BlockSpec((tm, FOX_WIDTH), lambda i: (i, REST_ZA // FOX_WIDTH)),
            pl.BlockSpec((tm, RET_WIDTH), lambda i: (i, 0)),
            pl.BlockSpec((tm, RET_WIDTH), lambda i: (i, REST_ZB // RET_WIDTH)),
            pl.BlockSpec((tm, D_MODEL), lambda i: (i, REST_GA // D_MODEL)),
            pl.BlockSpec((tm, D_MODEL), lambda i: (i, REST_GB // D_MODEL)),
            pl.BlockSpec((FOX_WIDTH, D_MODEL), lambda i: (0, 0), **const),
            pl.BlockSpec((RET_WIDTH, D_MODEL), lambda i: (0, 0), **const),
            pl.BlockSpec((D_MODEL, D_MODEL), lambda i: (0, 0), **const),
            pl.BlockSpec((1, D_MODEL), lambda i: (0, 0)),
        ],
        out_specs=pl.BlockSpec((tm, D_MODEL), lambda i: (i, 0)),
        out_shape=jax.ShapeDtypeStruct((m, D_MODEL), F32),
        compiler_params=_cparams(("arbitrary",)),
        name="merge",
    )(x, o_a, rest, o_b, rest, rest, rest, w_up_a, w_up_b, w_o, final_norm_w)


def _rope_tables(pos):
    half = RET_HEAD_DIM // 2
    inv = 1.0 / (ROPE_BASE ** (jnp.arange(half, dtype=F32) / half))
    ang = pos.astype(F32)[:, None] * inv[None, :]
    return jnp.cos(ang), jnp.sin(ang)


def _layer(xp, xs, cache_k, cache_v, cache_logf, state_ret, page_table,
           norm_w, w_in, b_f, ret_norm_w, w_up_a, w_up_b, w_o, final_norm_w):
    batch, seq, _ = xp.shape
    nb = xs.shape[0]
    past = page_table.shape[1] * PAGE_SIZE

    fw, rw = FOX_WIDTH, RET_WIDTH
    o_qa, o_ka, o_va, o_za, o_fa = 0, fw, 2 * fw, 3 * fw, 4 * fw
    o_qb = o_fa + FOX_HEADS
    o_kb, o_vb, o_zb, o_ga = o_qb + rw, o_qb + 2 * rw, o_qb + 3 * rw, o_qb + 4 * rw
    o_gb = o_ga + D_MODEL
    seg = lambda o, n: w_in[:, o:o + n]
    w_r = jnp.concatenate(
        [seg(o_ka, fw), seg(o_va, fw), seg(o_qa, fw), seg(o_za, fw), seg(o_qb, rw),
         seg(o_kb, rw), seg(o_vb, rw), seg(o_zb, rw), seg(o_ga, D_MODEL),
         seg(o_gb, D_MODEL)], axis=1).astype(BF16)
    w_f = jnp.pad(seg(o_fa, FOX_HEADS), ((0, 0), (0, LANES - FOX_HEADS))).astype(BF16)
    bf_pad = jnp.pad(b_f, (0, LANES - FOX_HEADS)).reshape(1, LANES)
    wua, wub, wo = w_up_a.astype(BF16), w_up_b.astype(BF16), w_o.astype(BF16)
    nw = norm_w.reshape(1, D_MODEL)
    rnw = ret_norm_w.reshape(1, RET_WIDTH)
    fnw = final_norm_w.reshape(1, D_MODEL)
    log_gamma = jnp.log(1.0 - 2.0 ** (-5.0 - jnp.arange(RET_HEADS, dtype=F32)))
    cos_p, sin_p = _rope_tables(jnp.arange(seq, dtype=jnp.int32))
    cos_s, sin_s = _rope_tables(past + jnp.arange(1, dtype=jnp.int32))

    x2 = xp.reshape(batch * seq, D_MODEL)
    k_p, v_p, rest_p, f_p = _inproj(x2, nw, w_r, w_f, tm=1024)
    logf_p, c_t = _logf_cumsum(f_p, bf_pad, batch, seq)
    o_a = _fox_prompt(rest_p, k_p, v_p, c_t, batch, seq)
    o_b, st_p = _ret_prompt(log_gamma, rest_p, cos_p, sin_p, rnw, batch, seq)
    y_p = _merge(x2, o_a, o_b, rest_p, wua, wub, wo, fnw, tm=256)

    xs2 = xs.reshape(nb, D_MODEL)
    k_s, v_s, rest_s, f_s = _inproj(xs2, nw, w_r, w_f, tm=nb)
    logf_s = _logf(f_s, bf_pad)
    rest_s3 = rest_s.reshape(nb, 1, REST_WIDTH)
    n_pool = cache_k.shape[0]
    o_a_s = _fox_decode(
        page_table, rest_s3, k_s.reshape(nb, 1, fw), v_s.reshape(nb, 1, fw),
        logf_s.reshape(nb, 1, FOX_HEADS),
        cache_k.reshape(n_pool, PAGE_SIZE, fw), cache_v.reshape(n_pool, PAGE_SIZE, fw),
        cache_logf)
    o_b_s, st_s = _ret_sample(log_gamma, rest_s3, cos_s, sin_s, rnw, state_ret)
    y_s = _merge(xs2, o_a_s.reshape(nb, fw), o_b_s.reshape(nb, rw), rest_s,
                 wua, wub, wo, fnw, tm=nb)

    hshape = (FOX_HEADS, FOX_HEAD_DIM)
    return (y_p.reshape(batch, seq, D_MODEL), y_s.reshape(nb, 1, D_MODEL),
            k_p.reshape(batch, seq, *hshape), v_p.reshape(batch, seq, *hshape),
            logf_p.reshape(batch, seq, FOX_HEADS), st_p,
            k_s.reshape(nb, 1, *hshape), v_s.reshape(nb, 1, *hshape),
            logf_s.reshape(nb, 1, FOX_HEADS), st_s)


def kernel(x_prompt, x_sample, cache_k, cache_v, cache_logf, state_ret, page_table,
           norm_w, w_in, b_f, ret_norm_w, w_up_a, w_up_b, w_o, final_norm_w):
    depth = w_in.shape[0]
    assert depth == 1 and x_sample.shape[1] == 1
    outs = _layer(x_prompt, x_sample, cache_k[0], cache_v[0], cache_logf[0], state_ret[0],
                  page_table, norm_w[0], w_in[0], b_f[0], ret_norm_w[0], w_up_a[0],
                  w_up_b[0], w_o[0], final_norm_w)
    y_p, y_s, k_p, v_p, lf_p, st_p, k_s, v_s, lf_s, st_s = outs
    stack = lambda a: a[None]
    return (y_p, y_s, stack(k_p), stack(v_p), stack(lf_p), stack(st_p),
            stack(k_s), stack(v_s), stack(lf_s), stack(st_s))
```

```python
import math

import jax
import jax.numpy as jnp
from jax import lax
from jax.experimental import pallas as pl
from jax.experimental.pallas import tpu as pltpu

F32 = jnp.float32
BF16 = jnp.bfloat16

D_MODEL = 2048
FOX_HEADS = 8
FOX_HEAD_DIM = 128
FOX_WIDTH = FOX_HEADS * FOX_HEAD_DIM
RET_HEADS = 4
RET_HEAD_DIM = 256
RET_WIDTH = RET_HEADS * RET_HEAD_DIM
RET_CHUNK = 128
PAGE_SIZE = 128
ROPE_BASE = 10000.0
EPS = 1e-6
NEG_INF = -1e30
FOX_SCALE = FOX_HEAD_DIM ** -0.5
RET_K_SCALE = RET_HEAD_DIM ** -0.5
LOG2E = math.log2(math.e)

LANES = 128
SUBLANES = 8
VMEM_LIMIT = 56 * 1024 * 1024

REST_QA, REST_ZA, REST_QB, REST_KB, REST_VB, REST_ZB, REST_GA, REST_GB = (
    0, 1024, 2048, 3072, 4096, 5120, 6144, 8192)
REST_WIDTH = 10240
PROJ_TN = 512
N_KV_TILES = FOX_WIDTH // PROJ_TN
F_ROWS = LANES

NT_DIMS = (((1,), (1,)), ((), ()))
TN_DIMS = (((0,), (0,)), ((), ()))


def _cparams(sem, vmem=VMEM_LIMIT):
    return pltpu.CompilerParams(dimension_semantics=sem, vmem_limit_bytes=vmem)


def _inproj_kernel(x_ref, nw_ref, wt_ref, wf_ref, k_ref, v_ref, rest_ref, f_ref, h_ref):
    j = pl.program_id(1)

    @pl.when(j == 0)
    def _():
        x = x_ref[...]
        ms = jnp.mean(x * x, axis=-1, keepdims=True)
        h = (x * lax.rsqrt(ms + EPS) * nw_ref[...]).astype(BF16)
        h_ref[...] = h
        f_ref[...] = lax.dot_general(h, wf_ref[...].astype(BF16), NT_DIMS,
                                     preferred_element_type=F32)

    acc = lax.dot_general(h_ref[...], wt_ref[...].astype(BF16), NT_DIMS,
                          preferred_element_type=F32)

    @pl.when(j < N_KV_TILES)
    def _():
        k_ref[...] = acc

    @pl.when((j >= N_KV_TILES) & (j < 2 * N_KV_TILES))
    def _():
        v_ref[...] = acc

    @pl.when(j >= 2 * N_KV_TILES)
    def _():
        rest_ref[...] = acc


def _wt_row_offset(j):
    t, n, fw = PROJ_TN, N_KV_TILES, FOX_WIDTH
    after_f = 4 * fw + FOX_HEADS
    return jnp.where(
        j < n, fw + j * t,
        jnp.where(j < 2 * n, 2 * fw + (j - n) * t,
                  jnp.where(j < 3 * n, (j - 2 * n) * t,
                            jnp.where(j < 4 * n, 3 * fw + (j - 3 * n) * t,
                                      after_f + (j - 4 * n) * t))))


def _inproj(x, norm_w, wt, tm):
    m = x.shape[0]
    n_tiles = (wt.shape[0] - FOX_HEADS) // PROJ_TN
    kv_last = N_KV_TILES - 1
    return pl.pallas_call(
        _inproj_kernel,
        grid=(m // tm, n_tiles),
        in_specs=[
            pl.BlockSpec((tm, D_MODEL), lambda i, j: (i, 0)),
            pl.BlockSpec((1, D_MODEL), lambda i, j: (0, 0)),
            pl.BlockSpec((pl.Element(PROJ_TN), pl.Element(D_MODEL)),
                         lambda i, j: (pl.multiple_of(_wt_row_offset(j), SUBLANES), 0)),
            pl.BlockSpec((F_ROWS, D_MODEL), lambda i, j: (4 * FOX_WIDTH // F_ROWS, 0)),
        ],
        out_specs=[
            pl.BlockSpec((tm, PROJ_TN), lambda i, j: (i, jnp.minimum(j, kv_last))),
            pl.BlockSpec((tm, PROJ_TN),
                         lambda i, j: (i, jnp.clip(j - N_KV_TILES, 0, kv_last))),
            pl.BlockSpec((tm, PROJ_TN),
                         lambda i, j: (i, jnp.maximum(j - 2 * N_KV_TILES, 0))),
            pl.BlockSpec((tm, F_ROWS), lambda i, j: (i, 0)),
        ],
        out_shape=[
            jax.ShapeDtypeStruct((m, FOX_WIDTH), F32),
            jax.ShapeDtypeStruct((m, FOX_WIDTH), F32),
            jax.ShapeDtypeStruct((m, REST_WIDTH), F32),
            jax.ShapeDtypeStruct((m, F_ROWS), F32),
        ],
        scratch_shapes=[pltpu.VMEM((tm, D_MODEL), BF16)],
        compiler_params=_cparams(("arbitrary", "arbitrary")),
        name="inproj",
    )(x, norm_w, wt, wt)


def _logf_kernel(f_ref, bf_ref, logf_ref):
    lf = jax.nn.log_sigmoid(f_ref[...] + bf_ref[...])
    logf_ref[...] = lf[:, :FOX_HEADS]


def _logf(f, bf_pad):
    m = f.shape[0]
    return pl.pallas_call(
        _logf_kernel,
        grid=(1,),
        in_specs=[pl.BlockSpec((m, F_ROWS), lambda i: (0, 0)),
                  pl.BlockSpec((1, F_ROWS), lambda i: (0, 0))],
        out_specs=pl.BlockSpec((m, FOX_HEADS), lambda i: (0, 0)),
        out_shape=jax.ShapeDtypeStruct((m, FOX_HEADS), F32),
        name="logf_sample",
    )(f, bf_pad)


def _logf_cumsum_kernel(f_ref, bf_ref, logf_ref, c_ref):
    seq = f_ref.shape[0]
    lf = jax.nn.log_sigmoid(f_ref[...] + bf_ref[...])
    logf_ref[...] = lf[:, :FOX_HEADS]
    lft = lf.T
    row = lax.broadcasted_iota(jnp.int32, (LANES, LANES), 0)
    col = lax.broadcasted_iota(jnp.int32, (LANES, LANES), 1)
    upper = (row <= col).astype(F32)
    carry = jnp.zeros((FOX_HEADS, 1), F32)
    for blk in range(seq // LANES):
        seg = lft[:FOX_HEADS, blk * LANES:(blk + 1) * LANES]
        cs = jnp.dot(seg, upper, preferred_element_type=F32,
                     precision=lax.Precision.HIGHEST) + carry
        c_ref[0, :, blk * LANES:(blk + 1) * LANES] = cs
        carry = cs[:, LANES - 1:LANES]


def _logf_cumsum(f, bf_pad, batch, seq):
    return pl.pallas_call(
        _logf_cumsum_kernel,
        grid=(batch,),
        in_specs=[pl.BlockSpec((seq, F_ROWS), lambda b: (b, 0)),
                  pl.BlockSpec((1, F_ROWS), lambda b: (0, 0))],
        out_specs=[pl.BlockSpec((seq, FOX_HEADS), lambda b: (b, 0)),
                   pl.BlockSpec((1, FOX_HEADS, seq), lambda b: (b, 0, 0))],
        out_shape=[jax.ShapeDtypeStruct((batch * seq, FOX_HEADS), F32),
                   jax.ShapeDtypeStruct((batch, FOX_HEADS, seq), F32)],
        compiler_params=_cparams(("arbitrary",)),
        name="logf_cumsum",
    )(f, bf_pad)


FOX_TQ = 512
FOX_TK = 512


def _fox_prompt_kernel(q_ref, k_ref, v_ref, c_ref, o_ref, kb_ref, vb_ref, c2_ref):
    h = pl.program_id(1)
    qi = pl.program_id(2)
    hd = FOX_HEAD_DIM

    @pl.when(qi == 0)
    def _():
        kb_ref[...] = k_ref[...].astype(BF16)
        vb_ref[:, :hd] = v_ref[...].astype(BF16)
        vb_ref[:, hd:] = jnp.ones((vb_ref.shape[0], hd), BF16)
        c2_ref[...] = c_ref[0, pl.ds(h, 1), :] * LOG2E

    q = q_ref[...].astype(BF16)

    def block(kbi, carry, masked):
        m, acc = carry
        start = pl.multiple_of(kbi * FOX_TK, FOX_TK)
        k = kb_ref[pl.ds(start, FOX_TK), :]
        v = vb_ref[pl.ds(start, FOX_TK), :]
        s = lax.dot_general(q, k, NT_DIMS, preferred_element_type=F32)
        s = s * (FOX_SCALE * LOG2E) - c2_ref[:, pl.ds(start, FOX_TK)]
        if masked:
            row = lax.broadcasted_iota(jnp.int32, (FOX_TQ, FOX_TK), 0)
            col = lax.broadcasted_iota(jnp.int32, (FOX_TQ, FOX_TK), 1)
            s = jnp.where(col <= row, s, NEG_INF)
        m_new = jnp.maximum(m, jnp.max(s, axis=-1, keepdims=True))
        alpha = jnp.exp2(m - m_new)
        p = jnp.exp2(s - m_new)
        acc = alpha * acc + jnp.dot(p.astype(BF16), v, preferred_element_type=F32)
        return m_new, acc

    init = (jnp.full((FOX_TQ, 1), NEG_INF, F32), jnp.zeros((FOX_TQ, 2 * hd), F32))
    carry = lax.fori_loop(0, qi, lambda kbi, c: block(kbi, c, False), init)
    _, acc = block(qi, carry, True)
    o_ref[...] = acc[:, :hd] / acc[:, hd:]


def _fox_prompt(rest, k, v, c_t, batch, seq):
    nq = seq // FOX_TQ
    hd = FOX_HEAD_DIM
    return pl.pallas_call(
        _fox_prompt_kernel,
        grid=(batch, FOX_HEADS, nq),
        in_specs=[
            pl.BlockSpec((FOX_TQ, hd), lambda b, h, qi: (b * nq + qi, REST_QA // hd + h)),
            pl.BlockSpec((seq, hd), lambda b, h, qi: (b, h)),
            pl.BlockSpec((seq, hd), lambda b, h, qi: (b, h)),
            pl.BlockSpec((1, FOX_HEADS, seq), lambda b, h, qi: (b, 0, 0)),
        ],
        out_specs=pl.BlockSpec((FOX_TQ, hd), lambda b, h, qi: (b * nq + qi, h)),
        out_shape=jax.ShapeDtypeStruct((batch * seq, FOX_WIDTH), F32),
        scratch_shapes=[pltpu.VMEM((seq, hd), BF16),
                        pltpu.VMEM((seq, 2 * hd), BF16),
                        pltpu.VMEM((1, seq), F32)],
        compiler_params=_cparams(("arbitrary", "arbitrary", "arbitrary")),
        name="fox_prompt",
    )(rest, k, v, c_t)


def _rotary(x, cos, sin):
    half = RET_HEAD_DIM // 2
    x1 = x[:, :half]
    x2 = x[:, half:]
    return jnp.concatenate([x1 * cos - x2 * sin, x2 * cos + x1 * sin], axis=-1)


def _head_rmsnorm(o, w):
    return o * lax.rsqrt(jnp.mean(o * o, axis=-1, keepdims=True) + EPS) * w


def _ret_prompt_kernel(lg_ref, q_ref, k_ref, v_ref, cos_ref, sin_ref, w_ref,
                       o_ref, st_ref):
    h = pl.program_id(1)
    lg = lg_ref[h]
    n_chunks = q_ref.shape[0] // RET_CHUNK
    ri = lax.broadcasted_iota(jnp.int32, (RET_CHUNK, RET_CHUNK), 0)
    ci = lax.broadcasted_iota(jnp.int32, (RET_CHUNK, RET_CHUNK), 1)
    diff = (ri - ci).astype(F32)
    decay = jnp.where(diff >= 0, jnp.exp(lg * jnp.maximum(diff, 0.0)), 0.0)
    pos = lax.broadcasted_iota(jnp.int32, (RET_CHUNK, RET_HEAD_DIM), 0).astype(F32)
    cross = jnp.exp(lg * (pos + 1.0))
    kdec = jnp.exp(lg * (RET_CHUNK - 1.0 - pos))
    g_chunk = jnp.exp(jnp.full((1, RET_HEAD_DIM), lg * RET_CHUNK, F32))
    w = w_ref[...]

    st_ref[0, 0] = jnp.zeros((RET_HEAD_DIM, RET_HEAD_DIM), F32)

    def chunk(c, _):
        r0 = pl.multiple_of(c * RET_CHUNK, RET_CHUNK)
        rows = pl.ds(r0, RET_CHUNK)
        cos = cos_ref[rows, :]
        sin = sin_ref[rows, :]
        qr = _rotary(q_ref[rows, :], cos, sin)
        kr = _rotary(k_ref[rows, :], cos, sin) * RET_K_SCALE
        qb = qr.astype(BF16)
        kb = kr.astype(BF16)
        vb = v_ref[rows, :].astype(BF16)
        st = st_ref[0, 0]
        scores = lax.dot_general(qb, kb, NT_DIMS, preferred_element_type=F32) * decay
        o = jnp.dot(scores.astype(BF16), vb, preferred_element_type=F32)
        o = o + cross * jnp.dot(qb, st.astype(BF16), preferred_element_type=F32)
        kd = (kr * kdec).astype(BF16)
        st_ref[0, 0] = g_chunk * st + lax.dot_general(kd, vb, TN_DIMS,
                                                      preferred_element_type=F32)
        o_ref[rows, :] = _head_rmsnorm(o, w)
        return 0

    lax.fori_loop(0, n_chunks, chunk, 0)


def _ret_prompt(log_gamma, rest, cos, sin, ret_norm_w, batch, seq):
    hd = RET_HEAD_DIM
    return pl.pallas_call(
        _ret_prompt_kernel,
        grid=(batch, RET_HEADS),
        in_specs=[
            pl.BlockSpec(memory_space=pltpu.SMEM),
            pl.BlockSpec((seq, hd), lambda b, h: (b, REST_QB // hd + h)),
            pl.BlockSpec((seq, hd), lambda b, h: (b, REST_KB // hd + h)),
            pl.BlockSpec((seq, hd), lambda b, h: (b, REST_VB // hd + h)),
            pl.BlockSpec((seq, hd // 2), lambda b, h: (0, 0)),
            pl.BlockSpec((seq, hd // 2), lambda b, h: (0, 0)),
            pl.BlockSpec((1, hd), lambda b, h: (0, h)),
        ],
        out_specs=[
            pl.BlockSpec((seq, hd), lambda b, h: (b, h)),
            pl.BlockSpec((1, 1, hd, hd), lambda b, h: (b, h, 0, 0)),
        ],
        out_shape=[
            jax.ShapeDtypeStruct((batch * seq, RET_WIDTH), F32),
            jax.ShapeDtypeStruct((batch, RET_HEADS, hd, hd), F32),
        ],
        compiler_params=_cparams(("arbitrary", "arbitrary")),
        name="ret_prompt",
    )(log_gamma, rest, rest, rest, cos, sin, ret_norm_w)


def _ret_sample_kernel(lg_ref, q_ref, k_ref, v_ref, cos_ref, sin_ref, w_ref, st_ref,
                       o_ref, nst_ref):
    hd = RET_HEAD_DIM
    cos = cos_ref[...]
    sin = sin_ref[...]
    first_row = lax.broadcasted_iota(jnp.int32, (16, hd), 0) == 0
    for h in range(RET_HEADS):
        cols = slice(h * hd, (h + 1) * hd)
        gamma = jnp.exp(jnp.full((1, hd), lg_ref[h], F32))
        qr = _rotary(q_ref[:, cols], cos, sin)
        kr = _rotary(k_ref[:, cols], cos, sin) * RET_K_SCALE
        vh = v_ref[:, cols]
        st = st_ref[0, h]
        q16 = jnp.broadcast_to(qr, (16, hd)).astype(BF16)
        qs = jnp.dot(q16, st.astype(BF16), preferred_element_type=F32)[0:1]
        score = jnp.sum(qr * kr, axis=-1, keepdims=True)
        o = score * vh + gamma * qs
        k16 = jnp.where(first_row, jnp.broadcast_to(kr, (16, hd)), 0.0).astype(BF16)
        v16 = jnp.broadcast_to(vh, (16, hd)).astype(BF16)
        nst_ref[0, h] = gamma * st + lax.dot_general(k16, v16, TN_DIMS,
                                                     preferred_element_type=F32)
        o_ref[:, cols] = _head_rmsnorm(o, w_ref[:, cols])


def _ret_sample(log_gamma, rest3, cos, sin, ret_norm_w, state):
    nb = rest3.shape[0]
    hd = RET_HEAD_DIM
    return pl.pallas_call(
        _ret_sample_kernel,
        grid=(nb,),
        in_specs=[
            pl.BlockSpec(memory_space=pltpu.SMEM),
            pl.BlockSpec((None, 1, RET_WIDTH), lambda b: (b, 0, REST_QB // RET_WIDTH)),
            pl.BlockSpec((None, 1, RET_WIDTH), lambda b: (b, 0, REST_KB // RET_WIDTH)),
            pl.BlockSpec((None, 1, RET_WIDTH), lambda b: (b, 0, REST_VB // RET_WIDTH)),
            pl.BlockSpec((1, hd // 2), lambda b: (0, 0)),
            pl.BlockSpec((1, hd // 2), lambda b: (0, 0)),
            pl.BlockSpec((1, RET_WIDTH), lambda b: (0, 0)),
            pl.BlockSpec((1, RET_HEADS, hd, hd), lambda b: (b, 0, 0, 0)),
        ],
        out_specs=[
            pl.BlockSpec((None, 1, RET_WIDTH), lambda b: (b, 0, 0)),
            pl.BlockSpec((1, RET_HEADS, hd, hd), lambda b: (b, 0, 0, 0)),
        ],
        out_shape=[
            jax.ShapeDtypeStruct((nb, 1, RET_WIDTH), F32),
            jax.ShapeDtypeStruct((nb, RET_HEADS, hd, hd), F32),
        ],
        compiler_params=_cparams(("arbitrary",)),
        name="ret_sample",
    )(log_gamma, rest3, rest3, rest3, cos, sin, ret_norm_w, state)


DEC_PAGES = 8
PAIRS = PAGE_SIZE // 2
PAIR_ROWS = PAIRS * FOX_HEADS
PAIR_WIDTH = 2 * FOX_HEAD_DIM
STREAMS = 2 * FOX_HEADS


def _split3(x):
    hi = x.astype(BF16)
    r1 = x - hi.astype(F32)
    mid = r1.astype(BF16)
    lo = (r1 - mid.astype(F32)).astype(BF16)
    return hi, mid, lo


def _pair_view(page_ref):
    even = page_ref[:, 0].reshape(PAIR_ROWS, FOX_HEAD_DIM)
    odd = page_ref[:, 1].reshape(PAIR_ROWS, FOX_HEAD_DIM)
    return jnp.concatenate([even, odd], axis=1).astype(BF16)


def _fox_decode_kernel(pt_ref, q_ref, kn_ref, vn_ref, lfn_ref, *refs):
    del pt_ref
    np_ = DEC_PAGES
    k_refs = refs[:np_]
    v_refs = refs[np_:2 * np_]
    lf_refs = refs[2 * np_:3 * np_]
    o_ref = refs[3 * np_]
    m_ref, l_ref, acc_ref, carry_ref, after_ref = refs[3 * np_ + 1:]
    b = pl.program_id(0)
    c = pl.program_id(1)
    n_chunks = pl.num_programs(1)
    hd = FOX_HEAD_DIM
    nh = FOX_HEADS

    @pl.when((b == 0) & (c == 0))
    def _():
        prow = lax.broadcasted_iota(jnp.int32, (PAGE_SIZE, 2 * PAIR_ROWS), 0)
        lane = lax.broadcasted_iota(jnp.int32, (PAGE_SIZE, 2 * PAIR_ROWS), 1)
        pos = 2 * ((lane % PAIR_ROWS) // nh) + lane // PAIR_ROWS
        after_ref[...] = jnp.where(prow > pos, 1.0, 0.0).astype(BF16)

    @pl.when(c == 0)
    def _():
        m_ref[...] = jnp.full((STREAMS, LANES), NEG_INF, F32)
        l_ref[...] = jnp.zeros((STREAMS, LANES), F32)
        acc_ref[...] = jnp.zeros((STREAMS, PAIR_WIDTH), F32)
        carry_ref[...] = jnp.zeros((nh, LANES), F32)

    q = q_ref[...]
    qb = q.astype(BF16)
    zq = jnp.zeros((nh, hd), BF16)
    qw = jnp.concatenate([jnp.concatenate([qb, zq], axis=1),
                          jnp.concatenate([zq, qb], axis=1)], axis=0)
    srow = lax.broadcasted_iota(jnp.int32, (STREAMS, PAIR_ROWS), 0)
    slane = lax.broadcasted_iota(jnp.int32, (STREAMS, PAIR_ROWS), 1)
    own_head = (slane % nh) == (srow % nh)

    lfs = jnp.concatenate([lf_refs[j][...] for j in range(np_)], axis=0)
    after = after_ref[...]
    suffix = sum(jnp.dot(piece, after, preferred_element_type=F32)
                 for piece in _split3(lfs))
    totals = jnp.sum(lfs, axis=1, keepdims=True)

    cn = lfn_ref[:, 0:1]
    carry = carry_ref[:, 0:1]
    s_pages = [None] * np_
    for j in reversed(range(np_)):
        rows = slice(j * nh, (j + 1) * nh)
        kw = _pair_view(k_refs[j])
        s = lax.dot_general(qw, kw, NT_DIMS, preferred_element_type=F32) * FOX_SCALE
        head_bias = carry + cn
        bias = jnp.concatenate([suffix[rows, :PAIR_ROWS] + head_bias,
                                suffix[rows, PAIR_ROWS:] + head_bias], axis=0)
        s_pages[j] = jnp.where(own_head, s + bias, NEG_INF)
        carry = carry + totals[rows]
    carry_ref[...] = jnp.broadcast_to(carry, (nh, LANES))

    m_old = m_ref[:, 0:1]
    m_new = m_old
    for j in range(np_):
        m_new = jnp.maximum(m_new, jnp.max(s_pages[j], axis=1, keepdims=True))
    alpha = jnp.exp(m_old - m_new)
    l_new = alpha * l_ref[:, 0:1]
    contrib = jnp.zeros((STREAMS, PAIR_WIDTH), F32)
    for j in range(np_):
        p = jnp.exp(s_pages[j] - m_new)
        l_new = l_new + jnp.sum(p, axis=1, keepdims=True)
        contrib = contrib + jnp.dot(p.astype(BF16), _pair_view(v_refs[j]),
                                    preferred_element_type=F32)
    acc = alpha * acc_ref[...] + contrib
    m_ref[...] = jnp.broadcast_to(m_new, (STREAMS, LANES))
    l_ref[...] = jnp.broadcast_to(l_new, (STREAMS, LANES))
    acc_ref[...] = acc

    @pl.when(c == n_chunks - 1)
    def _():
        sn = jnp.sum(q * kn_ref[...], axis=1, keepdims=True) * FOX_SCALE
        m_e, m_o = m_new[:nh], m_new[nh:]
        m_f = jnp.maximum(jnp.maximum(m_e, m_o), sn)
        w_e = jnp.exp(m_e - m_f)
        w_o = jnp.exp(m_o - m_f)
        p_n = jnp.exp(sn - m_f)
        denom = w_e * l_new[:nh] + w_o * l_new[nh:] + p_n
        out = w_e * acc[:nh, :hd] + w_o * acc[nh:, hd:] + p_n * vn_ref[...]
        o_ref[...] = out / denom


def _fox_decode(page_table, q3, kn3, vn3, lfn3, cache_k, cache_v, cache_lf_t):
    nb, n_pages = page_table.shape
    n_chunks = n_pages // DEC_PAGES
    nh, hd = FOX_HEADS, FOX_HEAD_DIM

    def page_map(j, ndim):
        def index_map(b, c, pt):
            return (pt[b, (n_chunks - 1 - c) * DEC_PAGES + j],) + (0,) * (ndim - 1)
        return index_map

    head_spec = pl.BlockSpec((None, nh, hd), lambda b, c, pt: (b, 0, 0))
    in_specs = [head_spec] * 4
    in_specs += [pl.BlockSpec((None, PAIRS, 2, nh, hd), page_map(j, 5))
                 for j in range(DEC_PAGES)] * 2
    in_specs += [pl.BlockSpec((None, nh, PAGE_SIZE), page_map(j, 3))
                 for j in range(DEC_PAGES)]
    grid_spec = pltpu.PrefetchScalarGridSpec(
        num_scalar_prefetch=1,
        grid=(nb, n_chunks),
        in_specs=in_specs,
        out_specs=head_spec,
        scratch_shapes=[
            pltpu.VMEM((STREAMS, LANES), F32),
            pltpu.VMEM((STREAMS, LANES), F32),
            pltpu.VMEM((STREAMS, PAIR_WIDTH), F32),
            pltpu.VMEM((nh, LANES), F32),
            pltpu.VMEM((PAGE_SIZE, 2 * PAIR_ROWS), BF16),
        ],
    )
    return pl.pallas_call(
        _fox_decode_kernel,
        grid_spec=grid_spec,
        out_shape=jax.ShapeDtypeStruct((nb, nh, hd), F32),
        compiler_params=_cparams(("arbitrary", "arbitrary")),
        name="fox_decode",
    )(page_table, q3, kn3, vn3, lfn3,
      *([cache_k] * DEC_PAGES), *([cache_v] * DEC_PAGES), *([cache_lf_t] * DEC_PAGES))


def _merge_kernel(x_ref, oa_ref, za_ref, ob_ref, zb_ref, ga_ref, gb_ref,
                  wua_ref, wub_ref, wo_ref, fnw_ref, y_ref):
    a = (oa_ref[...] * jax.nn.silu(za_ref[...])).astype(BF16)
    b = (ob_ref[...] * jax.nn.silu(zb_ref[...])).astype(BF16)
    ua = jnp.dot(a, wua_ref[...], preferred_element_type=F32)
    ub = jnp.dot(b, wub_ref[...], preferred_element_type=F32)
    merged = jax.nn.sigmoid(ga_ref[...]) * ua + jax.nn.sigmoid(gb_ref[...]) * ub
    out = x_ref[...] + jnp.dot(merged.astype(BF16), wo_ref[...], preferred_element_type=F32)
    ms = jnp.mean(out * out, axis=-1, keepdims=True)
    y_ref[...] = out * lax.rsqrt(ms + EPS) * fnw_ref[...]


def _merge(x, o_a, o_b, rest, w_up_a, w_up_b, w_o, final_norm_w, tm):
    m = x.shape[0]
    const = dict(pipeline_mode=pl.Buffered(1))
    return pl.pallas_call(
        _merge_kernel,
        grid=(m // tm,),
        in_specs=[
            pl.BlockSpec((tm, D_MODEL), lambda i: (i, 0)),
            pl.BlockSpec((tm, FOX_WIDTH), lambda i: (i, 0)),
            pl.BlockSpec((tm, FOX_WIDTH), lambda i: (i, REST_ZA // FOX_WIDTH)),
            pl.BlockSpec((tm, RET_WIDTH), lambda i: (i, 0)),
            pl.BlockSpec((tm, RET_WIDTH), lambda i: (i, REST_ZB // RET_WIDTH)),
            pl.BlockSpec((tm, D_MODEL), lambda i: (i, REST_GA // D_MODEL)),
            pl.BlockSpec((tm, D_MODEL), lambda i: (i, REST_GB // D_MODEL)),
            pl.BlockSpec((FOX_WIDTH, D_MODEL), lambda i: (0, 0), **const),
            pl.BlockSpec((RET_WIDTH, D_MODEL), lambda i: (0, 0), **const),
            pl.BlockSpec((D_MODEL, D_MODEL), lambda i: (0, 0), **const),
            pl.BlockSpec((1, D_MODEL), lambda i: (0, 0)),
        ],
        out_specs=pl.BlockSpec((tm, D_MODEL), lambda i: (i, 0)),
        out_shape=jax.ShapeDtypeStruct((m, D_MODEL), F32),
        compiler_params=_cparams(("arbitrary",)),
        name="merge",
    )(x, o_a, rest, o_b, rest, rest, rest, w_up_a, w_up_b, w_o, final_norm_w)


def _rope_tables(pos):
    half = RET_HEAD_DIM // 2
    inv = 1.0 / (ROPE_BASE ** (jnp.arange(half, dtype=F32) / half))
    ang = pos.astype(F32)[:, None] * inv[None, :]
    return jnp.cos(ang), jnp.sin(ang)


def _layer(xp, xs, cache_k, cache_v, cache_logf, state_ret, page_table,
           norm_w, w_in, b_f, ret_norm_w, w_up_a, w_up_b, w_o, final_norm_w):
    batch, seq, _ = xp.shape
    nb = xs.shape[0]
    n_pool = cache_k.shape[0]
    past = page_table.shape[1] * PAGE_SIZE
    nh, hd, fw, rw = FOX_HEADS, FOX_HEAD_DIM, FOX_WIDTH, RET_WIDTH

    wt = w_in.T
    bf_pad = jnp.pad(b_f, (0, F_ROWS - nh)).reshape(1, F_ROWS)
    wua, wub, wo = w_up_a.astype(BF16), w_up_b.astype(BF16), w_o.astype(BF16)
    nw = norm_w.reshape(1, D_MODEL)
    rnw = ret_norm_w.reshape(1, rw)
    fnw = final_norm_w.reshape(1, D_MODEL)
    log_gamma = jnp.log(1.0 - 2.0 ** (-5.0 - jnp.arange(RET_HEADS, dtype=F32)))
    cos_p, sin_p = _rope_tables(jnp.arange(seq, dtype=jnp.int32))
    cos_s, sin_s = _rope_tables(past + jnp.arange(1, dtype=jnp.int32))

    x2 = xp.reshape(batch * seq, D_MODEL)
    k_p, v_p, rest_p, f_p = _inproj(x2, nw, wt, tm=1024)
    logf_p, c_t = _logf_cumsum(f_p, bf_pad, batch, seq)
    o_a = _fox_prompt(rest_p, k_p, v_p, c_t, batch, seq)
    o_b, st_p = _ret_prompt(log_gamma, rest_p, cos_p, sin_p, rnw, batch, seq)
    y_p = _merge(x2, o_a, o_b, rest_p, wua, wub, wo, fnw, tm=256)

    xs2 = xs.reshape(nb, D_MODEL)
    k_s, v_s, rest_s, f_s = _inproj(xs2, nw, wt, tm=nb)
    logf_s = _logf(f_s, bf_pad)
    rest_s3 = rest_s.reshape(nb, 1, REST_WIDTH)
    q_s = rest_s[:, REST_QA:REST_QA + fw].reshape(nb, nh, hd)
    lfn = jnp.broadcast_to(logf_s[:, :, None], (nb, nh, hd))
    o_a_s = _fox_decode(
        page_table, q_s, k_s.reshape(nb, nh, hd), v_s.reshape(nb, nh, hd), lfn,
        cache_k.reshape(n_pool, PAIRS, 2, nh, hd), cache_v.reshape(n_pool, PAIRS, 2, nh, hd),
        jnp.swapaxes(cache_logf, 1, 2))
    o_b_s, st_s = _ret_sample(log_gamma, rest_s3, cos_s, sin_s, rnw, state_ret)
    y_s = _merge(xs2, o_a_s.reshape(nb, fw), o_b_s.reshape(nb, rw), rest_s,
                 wua, wub, wo, fnw, tm=nb)

    hshape = (nh, hd)
    return (y_p.reshape(batch, seq, D_MODEL), y_s.reshape(nb, 1, D_MODEL),
            k_p.reshape(batch, seq, *hshape), v_p.reshape(batch, seq, *hshape),
            logf_p.reshape(batch, seq, nh), st_p,
            k_s.reshape(nb, 1, *hshape), v_s.reshape(nb, 1, *hshape),
            logf_s.reshape(nb, 1, nh), st_s)


def kernel(x_prompt, x_sample, cache_k, cache_v, cache_logf, state_ret, page_table,
           norm_w, w_in, b_f, ret_norm_w, w_up_a, w_up_b, w_o, final_norm_w):
    depth = w_in.shape[0]
    assert depth == 1 and x_sample.shape[1] == 1
    outs = _layer(x_prompt, x_sample, cache_k[0], cache_v[0], cache_logf[0], state_ret[0],
                  page_table, norm_w[0], w_in[0], b_f[0], ret_norm_w[0], w_up_a[0],
                  w_up_b[0], w_o[0], final_norm_w)
    y_p, y_s, k_p, v_p, lf_p, st_p, k_s, v_s, lf_s, st_s = outs
    stack = lambda a: a[None]
    return (y_p, y_s, stack(k_p), stack(v_p), stack(lf_p), stack(st_p),
            stack(k_s), stack(v_s), stack(lf_s), stack(st_s))
```

```python
import math

import jax
import jax.numpy as jnp
from jax import lax
from jax.experimental import pallas as pl
from jax.experimental.pallas import tpu as pltpu

F32 = jnp.float32
BF16 = jnp.bfloat16

D_MODEL = 2048
FOX_HEADS = 8
FOX_HEAD_DIM = 128
FOX_WIDTH = FOX_HEADS * FOX_HEAD_DIM
RET_HEADS = 4
RET_HEAD_DIM = 256
RET_WIDTH = RET_HEADS * RET_HEAD_DIM
RET_CHUNK = 128
PAGE_SIZE = 128
ROPE_BASE = 10000.0
EPS = 1e-6
NEG_INF = -1e30
FOX_SCALE = FOX_HEAD_DIM ** -0.5
RET_K_SCALE = RET_HEAD_DIM ** -0.5
LOG2E = math.log2(math.e)

LANES = 128
SUBLANES = 8
VMEM_LIMIT = 56 * 1024 * 1024

REST_QA, REST_ZA, REST_QB, REST_KB, REST_VB, REST_ZB, REST_GA, REST_GB = (
    0, 1024, 2048, 3072, 4096, 5120, 6144, 8192)
REST_WIDTH = 10240
PROJ_TN = 512
N_KV_TILES = FOX_WIDTH // PROJ_TN
F_ROWS = LANES

NT_DIMS = (((1,), (1,)), ((), ()))
TN_DIMS = (((0,), (0,)), ((), ()))


def _cparams(sem, vmem=VMEM_LIMIT):
    return pltpu.CompilerParams(dimension_semantics=sem, vmem_limit_bytes=vmem)


def _inproj_kernel(x_ref, nw_ref, wt_ref, wf_ref, k_ref, v_ref, rest_ref, f_ref, h_ref):
    j = pl.program_id(1)

    @pl.when(j == 0)
    def _():
        x = x_ref[...]
        ms = jnp.mean(x * x, axis=-1, keepdims=True)
        h = (x * lax.rsqrt(ms + EPS) * nw_ref[...]).astype(BF16)
        h_ref[...] = h
        f_ref[...] = lax.dot_general(h, wf_ref[...].astype(BF16), NT_DIMS,
                                     preferred_element_type=F32)

    acc = lax.dot_general(h_ref[...], wt_ref[...].astype(BF16), NT_DIMS,
                          preferred_element_type=F32)

    @pl.when(j < N_KV_TILES)
    def _():
        k_ref[...] = acc

    @pl.when((j >= N_KV_TILES) & (j < 2 * N_KV_TILES))
    def _():
        v_ref[...] = acc

    @pl.when(j >= 2 * N_KV_TILES)
    def _():
        rest_ref[...] = acc


def _wt_row_offset(j):
    t, n, fw = PROJ_TN, N_KV_TILES, FOX_WIDTH
    after_f = 4 * fw + FOX_HEADS
    return jnp.where(
        j < n, fw + j * t,
        jnp.where(j < 2 * n, 2 * fw + (j - n) * t,
                  jnp.where(j < 3 * n, (j - 2 * n) * t,
                            jnp.where(j < 4 * n, 3 * fw + (j - 3 * n) * t,
                                      after_f + (j - 4 * n) * t))))


def _inproj(x, norm_w, wt, tm):
    m = x.shape[0]
    n_tiles = (wt.shape[0] - FOX_HEADS) // PROJ_TN
    kv_last = N_KV_TILES - 1
    return pl.pallas_call(
        _inproj_kernel,
        grid=(m // tm, n_tiles),
        in_specs=[
            pl.BlockSpec((tm, D_MODEL), lambda i, j: (i, 0)),
            pl.BlockSpec((1, D_MODEL), lambda i, j: (0, 0)),
            pl.BlockSpec((pl.Element(PROJ_TN), pl.Element(D_MODEL)),
                         lambda i, j: (pl.multiple_of(_wt_row_offset(j), SUBLANES), 0)),
            pl.BlockSpec((F_ROWS, D_MODEL), lambda i, j: (4 * FOX_WIDTH // F_ROWS, 0)),
        ],
        out_specs=[
            pl.BlockSpec((tm, PROJ_TN), lambda i, j: (i, jnp.minimum(j, kv_last))),
            pl.BlockSpec((tm, PROJ_TN),
                         lambda i, j: (i, jnp.clip(j - N_KV_TILES, 0, kv_last))),
            pl.BlockSpec((tm, PROJ_TN),
                         lambda i, j: (i, jnp.maximum(j - 2 * N_KV_TILES, 0))),
            pl.BlockSpec((tm, F_ROWS), lambda i, j: (i, 0)),
        ],
        out_shape=[
            jax.ShapeDtypeStruct((m, FOX_WIDTH), F32),
            jax.ShapeDtypeStruct((m, FOX_WIDTH), F32),
            jax.ShapeDtypeStruct((m, REST_WIDTH), F32),
            jax.ShapeDtypeStruct((m, F_ROWS), F32),
        ],
        scratch_shapes=[pltpu.VMEM((tm, D_MODEL), BF16)],
        compiler_params=_cparams(("arbitrary", "arbitrary")),
        name="inproj",
    )(x, norm_w, wt, wt)


def _logf_kernel(f_ref, bf_ref, logf_ref):
    lf = jax.nn.log_sigmoid(f_ref[...] + bf_ref[...])
    logf_ref[...] = lf[:, :FOX_HEADS]


def _logf(f, bf_pad):
    m = f.shape[0]
    return pl.pallas_call(
        _logf_kernel,
        grid=(1,),
        in_specs=[pl.BlockSpec((m, F_ROWS), lambda i: (0, 0)),
                  pl.BlockSpec((1, F_ROWS), lambda i: (0, 0))],
        out_specs=pl.BlockSpec((m, FOX_HEADS), lambda i: (0, 0)),
        out_shape=jax.ShapeDtypeStruct((m, FOX_HEADS), F32),
        name="logf_sample",
    )(f, bf_pad)


def _logf_cumsum_kernel(f_ref, bf_ref, logf_ref, c_ref):
    seq = f_ref.shape[0]
    lf = jax.nn.log_sigmoid(f_ref[...] + bf_ref[...])
    logf_ref[...] = lf[:, :FOX_HEADS]
    row = lax.broadcasted_iota(jnp.int32, (LANES, LANES), 0)
    col = lax.broadcasted_iota(jnp.int32, (LANES, LANES), 1)
    lower = (col <= row).astype(F32)
    carry = jnp.zeros((1, LANES), F32)
    for blk in range(seq // LANES):
        rows = slice(blk * LANES, (blk + 1) * LANES)
        cs = jnp.dot(lower, lf[rows, :], preferred_element_type=F32,
                     precision=lax.Precision.HIGHEST) + carry
        c_ref[rows, :] = cs
        carry = cs[LANES - 1:LANES, :]


def _logf_cumsum(f, bf_pad, batch, seq):
    return pl.pallas_call(
        _logf_cumsum_kernel,
        grid=(batch,),
        in_specs=[pl.BlockSpec((seq, F_ROWS), lambda b: (b, 0)),
                  pl.BlockSpec((1, F_ROWS), lambda b: (0, 0))],
        out_specs=[pl.BlockSpec((seq, FOX_HEADS), lambda b: (b, 0)),
                   pl.BlockSpec((seq, LANES), lambda b: (b, 0))],
        out_shape=[jax.ShapeDtypeStruct((batch * seq, FOX_HEADS), F32),
                   jax.ShapeDtypeStruct((batch * seq, LANES), F32)],
        compiler_params=_cparams(("arbitrary",)),
        name="logf_cumsum",
    )(f, bf_pad)


FOX_TQ = 512
FOX_TK = 512
N_BIAS_FEATURES = 3


def _split3(x):
    hi = x.astype(BF16)
    r1 = x - hi.astype(F32)
    mid = r1.astype(BF16)
    lo = (r1 - mid.astype(F32)).astype(BF16)
    return hi, mid, lo


def _fox_prompt_kernel(q_ref, k_ref, v_ref, c_ref, o_ref, kx_ref, vt_ref):
    h = pl.program_id(1)
    qi = pl.program_id(2)
    hd = FOX_HEAD_DIM
    seq = k_ref.shape[0]

    @pl.when(qi == 0)
    def _():
        lane = lax.broadcasted_iota(jnp.int32, (seq, LANES), 1)
        c_h = jnp.sum(jnp.where(lane == h, c_ref[...], 0.0), axis=1, keepdims=True)
        hi, mid, lo = (piece.astype(F32) for piece in _split3(
            jnp.broadcast_to(c_h * (1.0 / FOX_SCALE), (seq, LANES))))
        feat = jnp.where(lane == 0, hi, jnp.where(lane == 1, mid,
                                                  jnp.where(lane == 2, lo, 0.0)))
        kx_ref[:, :hd] = k_ref[...].astype(BF16)
        kx_ref[:, hd:] = feat.astype(BF16)
        vt_ref[:hd, :] = v_ref[...].T.astype(BF16)
        vt_ref[hd:, :] = jnp.ones((hd, seq), BF16)

    qlane = lax.broadcasted_iota(jnp.int32, (FOX_TQ, LANES), 1)
    qx = jnp.concatenate(
        [q_ref[...].astype(BF16),
         jnp.where(qlane < N_BIAS_FEATURES, -1.0, 0.0).astype(BF16)], axis=1)

    def block(kbi, carry, masked):
        m, acc = carry
        start = pl.multiple_of(kbi * FOX_TK, FOX_TK)
        kx = kx_ref[pl.ds(start, FOX_TK), :]
        s = lax.dot_general(kx, qx, NT_DIMS, preferred_element_type=F32)
        s = s * (FOX_SCALE * LOG2E)
        if masked:
            kpos = lax.broadcasted_iota(jnp.int32, (FOX_TK, FOX_TQ), 0)
            qpos = lax.broadcasted_iota(jnp.int32, (FOX_TK, FOX_TQ), 1)
            s = jnp.where(kpos <= qpos, s, NEG_INF)
        m_new = jnp.maximum(m, jnp.max(s, axis=0, keepdims=True))
        alpha = jnp.exp2(m - m_new)
        p = jnp.exp2(s - m_new)
        acc = alpha * acc + jnp.dot(vt_ref[:, pl.ds(start, FOX_TK)], p.astype(BF16),
                                    preferred_element_type=F32)
        return m_new, acc

    init = (jnp.full((1, FOX_TQ), NEG_INF, F32), jnp.zeros((2 * hd, FOX_TQ), F32))
    carry = lax.fori_loop(0, qi, lambda kbi, c: block(kbi, c, False), init)
    _, acc = block(qi, carry, True)
    o_ref[...] = (acc[:hd] / acc[hd:]).T


def _fox_prompt(rest, k, v, c, batch, seq):
    nq = seq // FOX_TQ
    hd = FOX_HEAD_DIM
    return pl.pallas_call(
        _fox_prompt_kernel,
        grid=(batch, FOX_HEADS, nq),
        in_specs=[
            pl.BlockSpec((FOX_TQ, hd), lambda b, h, qi: (b * nq + qi, REST_QA // hd + h)),
            pl.BlockSpec((seq, hd), lambda b, h, qi: (b, h)),
            pl.BlockSpec((seq, hd), lambda b, h, qi: (b, h)),
            pl.BlockSpec((seq, LANES), lambda b, h, qi: (b, 0)),
        ],
        out_specs=pl.BlockSpec((FOX_TQ, hd), lambda b, h, qi: (b * nq + qi, h)),
        out_shape=jax.ShapeDtypeStruct((batch * seq, FOX_WIDTH), F32),
        scratch_shapes=[pltpu.VMEM((seq, 2 * hd), BF16),
                        pltpu.VMEM((2 * hd, seq), BF16)],
        compiler_params=_cparams(("arbitrary", "arbitrary", "arbitrary")),
        name="fox_prompt",
    )(rest, k, v, c)


def _rotary(x, cos, sin):
    half = RET_HEAD_DIM // 2
    x1 = x[:, :half]
    x2 = x[:, half:]
    return jnp.concatenate([x1 * cos - x2 * sin, x2 * cos + x1 * sin], axis=-1)


def _head_rmsnorm(o, w):
    return o * lax.rsqrt(jnp.mean(o * o, axis=-1, keepdims=True) + EPS) * w


def _ret_prompt_kernel(lg_ref, q_ref, k_ref, v_ref, cos_ref, sin_ref, w_ref,
                       o_ref, st_ref):
    h = pl.program_id(1)
    lg = lg_ref[h]
    n_chunks = q_ref.shape[0] // RET_CHUNK
    ri = lax.broadcasted_iota(jnp.int32, (RET_CHUNK, RET_CHUNK), 0)
    ci = lax.broadcasted_iota(jnp.int32, (RET_CHUNK, RET_CHUNK), 1)
    diff = (ri - ci).astype(F32)
    decay = jnp.where(diff >= 0, jnp.exp(lg * jnp.maximum(diff, 0.0)), 0.0)
    pos = lax.broadcasted_iota(jnp.int32, (RET_CHUNK, RET_HEAD_DIM), 0).astype(F32)
    cross = jnp.exp(lg * (pos + 1.0))
    kdec = jnp.exp(lg * (RET_CHUNK - 1.0 - pos))
    g_chunk = jnp.exp(jnp.full((1, RET_HEAD_DIM), lg * RET_CHUNK, F32))
    w = w_ref[...]

    st_ref[0, 0] = jnp.zeros((RET_HEAD_DIM, RET_HEAD_DIM), F32)

    def chunk(c, _):
        r0 = pl.multiple_of(c * RET_CHUNK, RET_CHUNK)
        rows = pl.ds(r0, RET_CHUNK)
        cos = cos_ref[rows, :]
        sin = sin_ref[rows, :]
        qr = _rotary(q_ref[rows, :], cos, sin)
        kr = _rotary(k_ref[rows, :], cos, sin) * RET_K_SCALE
        qb = qr.astype(BF16)
        kb = kr.astype(BF16)
        vb = v_ref[rows, :].astype(BF16)
        st = st_ref[0, 0]
        scores = lax.dot_general(qb, kb, NT_DIMS, preferred_element_type=F32) * decay
        o = jnp.dot(scores.astype(BF16), vb, preferred_element_type=F32)
        o = o + cross * jnp.dot(qb, st.astype(BF16), preferred_element_type=F32)
        kd = (kr * kdec).astype(BF16)
        st_ref[0, 0] = g_chunk * st + lax.dot_general(kd, vb, TN_DIMS,
                                                      preferred_element_type=F32)
        o_ref[rows, :] = _head_rmsnorm(o, w)
        return 0

    lax.fori_loop(0, n_chunks, chunk, 0)


def _ret_prompt(log_gamma, rest, cos, sin, ret_norm_w, batch, seq):
    hd = RET_HEAD_DIM
    return pl.pallas_call(
        _ret_prompt_kernel,
        grid=(batch, RET_HEADS),
        in_specs=[
            pl.BlockSpec(memory_space=pltpu.SMEM),
            pl.BlockSpec((seq, hd), lambda b, h: (b, REST_QB // hd + h)),
            pl.BlockSpec((seq, hd), lambda b, h: (b, REST_KB // hd + h)),
            pl.BlockSpec((seq, hd), lambda b, h: (b, REST_VB // hd + h)),
            pl.BlockSpec((seq, hd // 2), lambda b, h: (0, 0)),
            pl.BlockSpec((seq, hd // 2), lambda b, h: (0, 0)),
            pl.BlockSpec((1, hd), lambda b, h: (0, h)),
        ],
        out_specs=[
            pl.BlockSpec((seq, hd), lambda b, h: (b, h)),
            pl.BlockSpec((1, 1, hd, hd), lambda b, h: (b, h, 0, 0)),
        ],
        out_shape=[
            jax.ShapeDtypeStruct((batch * seq, RET_WIDTH), F32),
            jax.ShapeDtypeStruct((batch, RET_HEADS, hd, hd), F32),
        ],
        compiler_params=_cparams(("arbitrary", "arbitrary")),
        name="ret_prompt",
    )(log_gamma, rest, rest, rest, cos, sin, ret_norm_w)


def _ret_sample_kernel(lg_ref, q_ref, k_ref, v_ref, cos_ref, sin_ref, w_ref, st_ref,
                       o_ref, nst_ref):
    hd = RET_HEAD_DIM
    cos = cos_ref[...]
    sin = sin_ref[...]
    first_row = lax.broadcasted_iota(jnp.int32, (16, hd), 0) == 0
    for h in range(RET_HEADS):
        cols = slice(h * hd, (h + 1) * hd)
        gamma = jnp.exp(jnp.full((1, hd), lg_ref[h], F32))
        qr = _rotary(q_ref[:, cols], cos, sin)
        kr = _rotary(k_ref[:, cols], cos, sin) * RET_K_SCALE
        vh = v_ref[:, cols]
        st = st_ref[0, h]
        q16 = jnp.broadcast_to(qr, (16, hd)).astype(BF16)
        qs = jnp.dot(q16, st.astype(BF16), preferred_element_type=F32)[0:1]
        score = jnp.sum(qr * kr, axis=-1, keepdims=True)
        o = score * vh + gamma * qs
        k16 = jnp.where(first_row, jnp.broadcast_to(kr, (16, hd)), 0.0).astype(BF16)
        v16 = jnp.broadcast_to(vh, (16, hd)).astype(BF16)
        nst_ref[0, h] = gamma * st + lax.dot_general(k16, v16, TN_DIMS,
                                                     preferred_element_type=F32)
        o_ref[:, cols] = _head_rmsnorm(o, w_ref[:, cols])


def _ret_sample(log_gamma, rest3, cos, sin, ret_norm_w, state):
    nb = rest3.shape[0]
    hd = RET_HEAD_DIM
    return pl.pallas_call(
        _ret_sample_kernel,
        grid=(nb,),
        in_specs=[
            pl.BlockSpec(memory_space=pltpu.SMEM),
            pl.BlockSpec((None, 1, RET_WIDTH), lambda b: (b, 0, REST_QB // RET_WIDTH)),
            pl.BlockSpec((None, 1, RET_WIDTH), lambda b: (b, 0, REST_KB // RET_WIDTH)),
            pl.BlockSpec((None, 1, RET_WIDTH), lambda b: (b, 0, REST_VB // RET_WIDTH)),
            pl.BlockSpec((1, hd // 2), lambda b: (0, 0)),
            pl.BlockSpec((1, hd // 2), lambda b: (0, 0)),
            pl.BlockSpec((1, RET_WIDTH), lambda b: (0, 0)),
            pl.BlockSpec((1, RET_HEADS, hd, hd), lambda b: (b, 0, 0, 0)),
        ],
        out_specs=[
            pl.BlockSpec((None, 1, RET_WIDTH), lambda b: (b, 0, 0)),
            pl.BlockSpec((1, RET_HEADS, hd, hd), lambda b: (b, 0, 0, 0)),
        ],
        out_shape=[
            jax.ShapeDtypeStruct((nb, 1, RET_WIDTH), F32),
            jax.ShapeDtypeStruct((nb, RET_HEADS, hd, hd), F32),
        ],
        compiler_params=_cparams(("arbitrary",)),
        name="ret_sample",
    )(log_gamma, rest3, rest3, rest3, cos, sin, ret_norm_w, state)


DEC_PAGES = 8
DEC_SLOTS = 2
DEC_GROUP = 2
PAIRS = PAGE_SIZE // 2
PAIR_ROWS = PAIRS * FOX_HEADS
PAIR_WIDTH = 2 * FOX_HEAD_DIM
STREAMS = 2 * FOX_HEADS
SEM_K, SEM_V, SEM_LF = 0, 1, 2


def _pair_view(page_ref):
    even = page_ref[:, 0].reshape(PAIR_ROWS, FOX_HEAD_DIM)
    odd = page_ref[:, 1].reshape(PAIR_ROWS, FOX_HEAD_DIM)
    return jnp.concatenate([even, odd], axis=1).astype(BF16)


def _fox_decode_kernel(pt_ref, q_ref, kn_ref, vn_ref, lfn_ref, ck_hbm, cv_hbm, clf_hbm,
                       o_ref, kbuf, vbuf, lfbuf, sems, m_ref, l_ref, acc_ref, carry_ref,
                       after_ref):
    np_ = DEC_PAGES
    hd = FOX_HEAD_DIM
    nh = FOX_HEADS
    b = pl.program_id(0)
    i = pl.program_id(1)
    steps_per_seq = pl.num_programs(1)
    chunks_per_seq = DEC_SLOTS * steps_per_seq
    total_chunks = pl.num_programs(0) * chunks_per_seq
    first_chunk = (b * steps_per_seq + i) * DEC_SLOTS

    def chunk_copies(chunk, slot):
        seq_id = chunk // chunks_per_seq
        first_page = (chunks_per_seq - 1 - chunk % chunks_per_seq) * np_
        copies = []
        for j in range(np_):
            page = pt_ref[seq_id, first_page + j]
            copies.append(pltpu.make_async_copy(ck_hbm.at[page], kbuf.at[slot, j],
                                                sems.at[slot, SEM_K]))
            copies.append(pltpu.make_async_copy(cv_hbm.at[page], vbuf.at[slot, j],
                                                sems.at[slot, SEM_V]))
            copies.append(pltpu.make_async_copy(clf_hbm.at[page], lfbuf.at[slot, j],
                                                sems.at[slot, SEM_LF]))
        return copies

    def start_chunk(chunk, slot):
        for cp in chunk_copies(chunk, slot):
            cp.start()

    def wait_chunk(chunk, slot):
        for cp in chunk_copies(chunk, slot):
            cp.wait()

    @pl.when((b == 0) & (i == 0))
    def _():
        prow = lax.broadcasted_iota(jnp.int32, (PAGE_SIZE, 2 * PAIR_ROWS), 0)
        lane = lax.broadcasted_iota(jnp.int32, (PAGE_SIZE, 2 * PAIR_ROWS), 1)
        pos = 2 * ((lane % PAIR_ROWS) // nh) + lane // PAIR_ROWS
        after_ref[...] = jnp.where(prow > pos, 1.0, 0.0).astype(BF16)
        for slot in range(DEC_SLOTS):
            start_chunk(slot, slot)

    @pl.when(i == 0)
    def _():
        m_ref[...] = jnp.full((STREAMS, LANES), NEG_INF, F32)
        l_ref[...] = jnp.zeros((STREAMS, LANES), F32)
        acc_ref[...] = jnp.zeros((STREAMS, PAIR_WIDTH), F32)
        carry_ref[...] = jnp.zeros((nh, LANES), F32)

    q = q_ref[...]
    qb = q.astype(BF16)
    zq = jnp.zeros((nh, hd), BF16)
    qw = jnp.concatenate([jnp.concatenate([qb, zq], axis=1),
                          jnp.concatenate([zq, qb], axis=1)], axis=0)
    srow = lax.broadcasted_iota(jnp.int32, (STREAMS, PAIR_ROWS), 0)
    slane = lax.broadcasted_iota(jnp.int32, (STREAMS, PAIR_ROWS), 1)
    own_head = (slane % nh) == (srow % nh)
    cn = lfn_ref[:, 0:1]

    def consume(slot, state):
        m, l, acc, carry = state
        lfs = lfbuf[slot].reshape(np_ * nh, PAGE_SIZE)
        after = after_ref[...]
        suffix = sum(jnp.dot(piece, after, preferred_element_type=F32)
                     for piece in _split3(lfs))
        totals = jnp.sum(lfs, axis=1, keepdims=True)
        head_bias = [None] * np_
        for j in reversed(range(np_)):
            head_bias[j] = carry + cn
            carry = carry + totals[j * nh:(j + 1) * nh]

        def logits(j):
            rows = slice(j * nh, (j + 1) * nh)
            kw = _pair_view(kbuf.at[slot, j])
            s = lax.dot_general(qw, kw, NT_DIMS, preferred_element_type=F32) * FOX_SCALE
            bias = jnp.concatenate([suffix[rows, :PAIR_ROWS] + head_bias[j],
                                    suffix[rows, PAIR_ROWS:] + head_bias[j]], axis=0)
            return jnp.where(own_head, s + bias, NEG_INF)

        def update(m, l, acc, pages, s_pages):
            m_new = m
            for s in s_pages:
                m_new = jnp.maximum(m_new, jnp.max(s, axis=1, keepdims=True))
            alpha = jnp.exp(m - m_new)
            l = alpha * l
            acc = alpha * acc
            for j, s in zip(pages, s_pages):
                p = jnp.exp(s - m_new)
                l = l + jnp.sum(p, axis=1, keepdims=True)
                acc = acc + jnp.dot(p.astype(BF16), _pair_view(vbuf.at[slot, j]),
                                    preferred_element_type=F32)
            return m_new, l, acc

        groups = [list(range(g + DEC_GROUP - 1, g - 1, -1))
                  for g in range(np_ - DEC_GROUP, -1, -DEC_GROUP)]
        s_next = [logits(j) for j in groups[0]]
        for g, pages in enumerate(groups):
            s_cur = s_next
            if g + 1 < len(groups):
                s_next = [logits(j) for j in groups[g + 1]]
            m, l, acc = update(m, l, acc, pages, s_cur)
        return m, l, acc, carry

    state = (m_ref[:, 0:1], l_ref[:, 0:1], acc_ref[...], carry_ref[:, 0:1])
    for slot in range(DEC_SLOTS):
        wait_chunk(first_chunk + slot, slot)
        state = consume(slot, state)
        refill = first_chunk + slot + DEC_SLOTS

        @pl.when(refill < total_chunks)
        def _():
            start_chunk(refill, slot)

    m, l, acc, carry = state
    m_ref[...] = jnp.broadcast_to(m, (STREAMS, LANES))
    l_ref[...] = jnp.broadcast_to(l, (STREAMS, LANES))
    acc_ref[...] = acc
    carry_ref[...] = jnp.broadcast_to(carry, (nh, LANES))

    @pl.when(i == steps_per_seq - 1)
    def _():
        sn = jnp.sum(q * kn_ref[...], axis=1, keepdims=True) * FOX_SCALE
        m_e, m_o = m[:nh], m[nh:]
        m_f = jnp.maximum(jnp.maximum(m_e, m_o), sn)
        w_e = jnp.exp(m_e - m_f)
        w_o = jnp.exp(m_o - m_f)
        p_n = jnp.exp(sn - m_f)
        denom = w_e * l[:nh] + w_o * l[nh:] + p_n
        out = w_e * acc[:nh, :hd] + w_o * acc[nh:, hd:] + p_n * vn_ref[...]
        o_ref[...] = out / denom


def _fox_decode(page_table, q3, kn3, vn3, lfn3, cache_k, cache_v, cache_lf_t):
    nb, n_pages = page_table.shape
    steps_per_seq = n_pages // (DEC_PAGES * DEC_SLOTS)
    nh, hd = FOX_HEADS, FOX_HEAD_DIM
    head_spec = pl.BlockSpec((None, nh, hd), lambda b, i, pt: (b, 0, 0))
    hbm_spec = pl.BlockSpec(memory_space=pl.ANY)
    grid_spec = pltpu.PrefetchScalarGridSpec(
        num_scalar_prefetch=1,
        grid=(nb, steps_per_seq),
        in_specs=[head_spec] * 4 + [hbm_spec] * 3,
        out_specs=head_spec,
        scratch_shapes=[
            pltpu.VMEM((DEC_SLOTS, DEC_PAGES, PAIRS, 2, nh, hd), F32),
            pltpu.VMEM((DEC_SLOTS, DEC_PAGES, PAIRS, 2, nh, hd), F32),
            pltpu.VMEM((DEC_SLOTS, DEC_PAGES, nh, PAGE_SIZE), F32),
            pltpu.SemaphoreType.DMA((DEC_SLOTS, 3)),
            pltpu.VMEM((STREAMS, LANES), F32),
            pltpu.VMEM((STREAMS, LANES), F32),
            pltpu.VMEM((STREAMS, PAIR_WIDTH), F32),
            pltpu.VMEM((nh, LANES), F32),
            pltpu.VMEM((PAGE_SIZE, 2 * PAIR_ROWS), BF16),
        ],
    )
    return pl.pallas_call(
        _fox_decode_kernel,
        grid_spec=grid_spec,
        out_shape=jax.ShapeDtypeStruct((nb, nh, hd), F32),
        compiler_params=_cparams(("arbitrary", "arbitrary")),
        name="fox_decode",
    )(page_table, q3, kn3, vn3, lfn3, cache_k, cache_v, cache_lf_t)


def _merge_kernel(x_ref, oa_ref, za_ref, ob_ref, zb_ref, ga_ref, gb_ref,
                  wua_ref, wub_ref, wo_ref, fnw_ref, y_ref):
    a = (oa_ref[...] * jax.nn.silu(za_ref[...])).astype(BF16)
    b = (ob_ref[...] * jax.nn.silu(zb_ref[...])).astype(BF16)
    ua = jnp.dot(a, wua_ref[...], preferred_element_type=F32)
    ub = jnp.dot(b, wub_ref[...], preferred_element_type=F32)
    merged = jax.nn.sigmoid(ga_ref[...]) * ua + jax.nn.sigmoid(gb_ref[...]) * ub
    out = x_ref[...] + jnp.dot(merged.astype(BF16), wo_ref[...], preferred_element_type=F32)
    ms = jnp.mean(out * out, axis=-1, keepdims=True)
    y_ref[...] = out * lax.rsqrt(ms + EPS) * fnw_ref[...]


def _merge(x, o_a, o_b, rest, w_up_a, w_up_b, w_o, final_norm_w, tm):
    m = x.shape[0]
    const = dict(pipeline_mode=pl.Buffered(1))
    return pl.pallas_call(
        _merge_kernel,
        grid=(m // tm,),
        in_specs=[
            pl.BlockSpec((tm, D_MODEL), lambda i: (i, 0)),
            pl.BlockSpec((tm, FOX_WIDTH), lambda i: (i, 0)),
            pl.BlockSpec((tm, FOX_WIDTH), lambda i: (i, REST_ZA // FOX_WIDTH)),
            pl.BlockSpec((tm, RET_WIDTH), lambda i: (i, 0)),
            pl.BlockSpec((tm, RET_WIDTH), lambda i: (i, REST_ZB // RET_WIDTH)),
            pl.BlockSpec((tm, D_MODEL), lambda i: (i, REST_GA // D_MODEL)),
            pl.BlockSpec((tm, D_MODEL), lambda i: (i, REST_GB // D_MODEL)),
            pl.BlockSpec((FOX_WIDTH, D_MODEL), lambda i: (0, 0), **const),
            pl.BlockSpec((RET_WIDTH, D_MODEL), lambda i: (0, 0), **const),
            pl.BlockSpec((D_MODEL, D_MODEL), lambda i: (0, 0), **const),
            pl.BlockSpec((1, D_MODEL), lambda i: (0, 0)),
        ],
        out_specs=pl.BlockSpec((tm, D_MODEL), lambda i: (i, 0)),
        out_shape=jax.ShapeDtypeStruct((m, D_MODEL), F32),
        compiler_params=_cparams(("arbitrary",)),
        name="merge",
    )(x, o_a, rest, o_b, rest, rest, rest, w_up_a, w_up_b, w_o, final_norm_w)


def _rope_tables(pos):
    half = RET_HEAD_DIM // 2
    inv = 1.0 / (ROPE_BASE ** (jnp.arange(half, dtype=F32) / half))
    ang = pos.astype(F32)[:, None] * inv[None, :]
    return jnp.cos(ang), jnp.sin(ang)


def _layer(xp, xs, cache_k, cache_v, cache_logf, state_ret, page_table,
           norm_w, w_in, b_f, ret_norm_w, w_up_a, w_up_b, w_o, final_norm_w):
    batch, seq, _ = xp.shape
    nb = xs.shape[0]
    n_pool = cache_k.shape[0]
    past = page_table.shape[1] * PAGE_SIZE
    nh, hd, fw, rw = FOX_HEADS, FOX_HEAD_DIM, FOX_WIDTH, RET_WIDTH

    wt = w_in.T
    bf_pad = jnp.pad(b_f, (0, F_ROWS - nh)).reshape(1, F_ROWS)
    wua, wub, wo = w_up_a.astype(BF16), w_up_b.astype(BF16), w_o.astype(BF16)
    nw = norm_w.reshape(1, D_MODEL)
    rnw = ret_norm_w.reshape(1, rw)
    fnw = final_norm_w.reshape(1, D_MODEL)
    log_gamma = jnp.log(1.0 - 2.0 ** (-5.0 - jnp.arange(RET_HEADS, dtype=F32)))
    cos_p, sin_p = _rope_tables(jnp.arange(seq, dtype=jnp.int32))
    cos_s, sin_s = _rope_tables(past + jnp.arange(1, dtype=jnp.int32))

    x2 = xp.reshape(batch * seq, D_MODEL)
    k_p, v_p, rest_p, f_p = _inproj(x2, nw, wt, tm=1024)
    logf_p, c_p = _logf_cumsum(f_p, bf_pad, batch, seq)
    o_a = _fox_prompt(rest_p, k_p, v_p, c_p, batch, seq)
    o_b, st_p = _ret_prompt(log_gamma, rest_p, cos_p, sin_p, rnw, batch, seq)
    y_p = _merge(x2, o_a, o_b, rest_p, wua, wub, wo, fnw, tm=256)

    xs2 = xs.reshape(nb, D_MODEL)
    k_s, v_s, rest_s, f_s = _inproj(xs2, nw, wt, tm=nb)
    logf_s = _logf(f_s, bf_pad)
    rest_s3 = rest_s.reshape(nb, 1, REST_WIDTH)
    q_s = rest_s[:, REST_QA:REST_QA + fw].reshape(nb, nh, hd)
    lfn = jnp.broadcast_to(logf_s[:, :, None], (nb, nh, hd))
    o_a_s = _fox_decode(
        page_table, q_s, k_s.reshape(nb, nh, hd), v_s.reshape(nb, nh, hd), lfn,
        cache_k.reshape(n_pool, PAIRS, 2, nh, hd), cache_v.reshape(n_pool, PAIRS, 2, nh, hd),
        jnp.swapaxes(cache_logf, 1, 2))
    o_b_s, st_s = _ret_sample(log_gamma, rest_s3, cos_s, sin_s, rnw, state_ret)
    y_s = _merge(xs2, o_a_s.reshape(nb, fw), o_b_s.reshape(nb, rw), rest_s,
                 wua, wub, wo, fnw, tm=nb)

    hshape = (nh, hd)
    return (y_p.reshape(batch, seq, D_MODEL), y_s.reshape(nb, 1, D_MODEL),
            k_p.reshape(batch, seq, *hshape), v_p.reshape(batch, seq, *hshape),
            logf_p.reshape(batch, seq, nh), st_p,
            k_s.reshape(nb, 1, *hshape), v_s.reshape(nb, 1, *hshape),
            logf_s.reshape(nb, 1, nh), st_s)


def kernel(x_prompt, x_sample, cache_k, cache_v, cache_logf, state_ret, page_table,
           norm_w, w_in, b_f, ret_norm_w, w_up_a, w_up_b, w_o, final_norm_w):
    depth = w_in.shape[0]
    assert depth == 1 and x_sample.shape[1] == 1
    outs = _layer(x_prompt, x_sample, cache_k[0], cache_v[0], cache_logf[0], state_ret[0],
                  page_table, norm_w[0], w_in[0], b_f[0], ret_norm_w[0], w_up_a[0],
                  w_up_b[0], w_o[0], final_norm_w)
    y_p, y_s, k_p, v_p, lf_p, st_p, k_s, v_s, lf_s, st_s = outs
    stack = lambda a: a[None]
    return (y_p, y_s, stack(k_p), stack(v_p), stack(lf_p), stack(st_p),
            stack(k_s), stack(v_s), stack(lf_s), stack(st_s))
```

```python
import functools
import math

import jax
import jax.numpy as jnp
from jax import lax
from jax.experimental import pallas as pl
from jax.experimental.pallas import tpu as pltpu

F32 = jnp.float32
BF16 = jnp.bfloat16

D_MODEL = 2048
FOX_HEADS = 8
FOX_HEAD_DIM = 128
FOX_WIDTH = FOX_HEADS * FOX_HEAD_DIM
RET_HEADS = 4
RET_HEAD_DIM = 256
RET_WIDTH = RET_HEADS * RET_HEAD_DIM
RET_CHUNK = 128
PAGE_SIZE = 128
ROPE_BASE = 10000.0
EPS = 1e-6
NEG_INF = -1e30
FOX_SCALE = FOX_HEAD_DIM ** -0.5
RET_K_SCALE = RET_HEAD_DIM ** -0.5
LOG2E = math.log2(math.e)

LANES = 128
SUBLANES = 8
VMEM_LIMIT = 56 * 1024 * 1024

REST_QA, REST_ZA, REST_QB, REST_KB, REST_VB, REST_ZB, REST_GA, REST_GB = (
    0, 1024, 2048, 3072, 4096, 5120, 6144, 8192)
REST_WIDTH = 10240
PROJ_TN = 512
N_KV_TILES = FOX_WIDTH // PROJ_TN
F_ROWS = LANES

NT_DIMS = (((1,), (1,)), ((), ()))
TN_DIMS = (((0,), (0,)), ((), ()))


def _cparams(sem, vmem=VMEM_LIMIT):
    return pltpu.CompilerParams(dimension_semantics=sem, vmem_limit_bytes=vmem)


def _inproj_kernel(x_ref, nw_ref, wt_ref, wf_ref, k_ref, v_ref, rest_ref, f_ref, h_ref, *,
                   weights_as_lhs):
    j = pl.program_id(1)
    m = x_ref.shape[0]

    def project(w_ref):
        w = w_ref[...].astype(BF16)
        if weights_as_lhs:
            out_t = lax.dot_general(w, h_ref[...], NT_DIMS, preferred_element_type=F32)
            return out_t.T[:m]
        return lax.dot_general(h_ref[...], w, NT_DIMS, preferred_element_type=F32)

    @pl.when(j == 0)
    def _():
        x = x_ref[...]
        ms = jnp.mean(x * x, axis=-1, keepdims=True)
        h = (x * lax.rsqrt(ms + EPS) * nw_ref[...]).astype(BF16)
        if h_ref.shape[0] != m:
            h_ref[...] = jnp.zeros(h_ref.shape, BF16)
        h_ref[:m, :] = h
        f_ref[...] = project(wf_ref)

    acc = project(wt_ref)

    @pl.when(j < N_KV_TILES)
    def _():
        k_ref[...] = acc

    @pl.when((j >= N_KV_TILES) & (j < 2 * N_KV_TILES))
    def _():
        v_ref[...] = acc

    @pl.when(j >= 2 * N_KV_TILES)
    def _():
        rest_ref[...] = acc.astype(rest_ref.dtype)


def _wt_row_offset(j):
    t, n, fw = PROJ_TN, N_KV_TILES, FOX_WIDTH
    after_f = 4 * fw + FOX_HEADS
    return jnp.where(
        j < n, fw + j * t,
        jnp.where(j < 2 * n, 2 * fw + (j - n) * t,
                  jnp.where(j < 3 * n, (j - 2 * n) * t,
                            jnp.where(j < 4 * n, 3 * fw + (j - 3 * n) * t,
                                      after_f + (j - 4 * n) * t))))


def _inproj(x, norm_w, wt, tm, rest_dtype):
    m = x.shape[0]
    n_tiles = (wt.shape[0] - FOX_HEADS) // PROJ_TN
    kv_last = N_KV_TILES - 1
    weights_as_lhs = tm < LANES
    return pl.pallas_call(
        functools.partial(_inproj_kernel, weights_as_lhs=weights_as_lhs),
        grid=(m // tm, n_tiles),
        in_specs=[
            pl.BlockSpec((tm, D_MODEL), lambda i, j: (i, 0)),
            pl.BlockSpec((1, D_MODEL), lambda i, j: (0, 0)),
            pl.BlockSpec((pl.Element(PROJ_TN), pl.Element(D_MODEL)),
                         lambda i, j: (pl.multiple_of(_wt_row_offset(j), SUBLANES), 0)),
            pl.BlockSpec((F_ROWS, D_MODEL), lambda i, j: (4 * FOX_WIDTH // F_ROWS, 0)),
        ],
        out_specs=[
            pl.BlockSpec((tm, PROJ_TN), lambda i, j: (i, jnp.minimum(j, kv_last))),
            pl.BlockSpec((tm, PROJ_TN),
                         lambda i, j: (i, jnp.clip(j - N_KV_TILES, 0, kv_last))),
            pl.BlockSpec((tm, PROJ_TN),
                         lambda i, j: (i, jnp.maximum(j - 2 * N_KV_TILES, 0))),
            pl.BlockSpec((tm, F_ROWS), lambda i, j: (i, 0)),
        ],
        out_shape=[
            jax.ShapeDtypeStruct((m, FOX_WIDTH), F32),
            jax.ShapeDtypeStruct((m, FOX_WIDTH), F32),
            jax.ShapeDtypeStruct((m, REST_WIDTH), rest_dtype),
            jax.ShapeDtypeStruct((m, F_ROWS), F32),
        ],
        scratch_shapes=[pltpu.VMEM((max(tm, LANES), D_MODEL), BF16)],
        compiler_params=_cparams(("arbitrary", "arbitrary")),
        name="inproj",
    )(x, norm_w, wt, wt)


def _logf_kernel(f_ref, bf_ref, logf_ref):
    lf = jax.nn.log_sigmoid(f_ref[...] + bf_ref[...])
    logf_ref[...] = lf[:, :FOX_HEADS]


def _logf(f, bf_pad):
    m = f.shape[0]
    return pl.pallas_call(
        _logf_kernel,
        grid=(1,),
        in_specs=[pl.BlockSpec((m, F_ROWS), lambda i: (0, 0)),
                  pl.BlockSpec((1, F_ROWS), lambda i: (0, 0))],
        out_specs=pl.BlockSpec((m, FOX_HEADS), lambda i: (0, 0)),
        out_shape=jax.ShapeDtypeStruct((m, FOX_HEADS), F32),
        name="logf_sample",
    )(f, bf_pad)


def _logf_cumsum_kernel(f_ref, bf_ref, logf_ref, c_ref):
    seq = f_ref.shape[0]
    lf = jax.nn.log_sigmoid(f_ref[...] + bf_ref[...])
    logf_ref[...] = lf[:, :FOX_HEADS]
    row = lax.broadcasted_iota(jnp.int32, (LANES, LANES), 0)
    col = lax.broadcasted_iota(jnp.int32, (LANES, LANES), 1)
    lower = (col <= row).astype(F32)
    carry = jnp.zeros((1, LANES), F32)
    for blk in range(seq // LANES):
        rows = slice(blk * LANES, (blk + 1) * LANES)
        cs = jnp.dot(lower, lf[rows, :], preferred_element_type=F32,
                     precision=lax.Precision.HIGHEST) + carry
        c_ref[rows, :] = cs
        carry = cs[LANES - 1:LANES, :]


def _logf_cumsum(f, bf_pad, batch, seq):
    return pl.pallas_call(
        _logf_cumsum_kernel,
        grid=(batch,),
        in_specs=[pl.BlockSpec((seq, F_ROWS), lambda b: (b, 0)),
                  pl.BlockSpec((1, F_ROWS), lambda b: (0, 0))],
        out_specs=[pl.BlockSpec((seq, FOX_HEADS), lambda b: (b, 0)),
                   pl.BlockSpec((seq, LANES), lambda b: (b, 0))],
        out_shape=[jax.ShapeDtypeStruct((batch * seq, FOX_HEADS), F32),
                   jax.ShapeDtypeStruct((batch * seq, LANES), F32)],
        compiler_params=_cparams(("arbitrary",)),
        name="logf_cumsum",
    )(f, bf_pad)


FOX_TQ = 512
FOX_TK = 512
FOX_HPS = 2
N_BIAS_FEATURES = 3


def _split3(x):
    hi = x.astype(BF16)
    r1 = x - hi.astype(F32)
    mid = r1.astype(BF16)
    lo = (r1 - mid.astype(F32)).astype(BF16)
    return hi, mid, lo


def _fox_prompt_kernel(q_ref, k_ref, v_ref, c_ref, o_ref, kx_ref, vt_ref):
    hg = pl.program_id(1)
    qi = pl.program_id(2)
    hd = FOX_HEAD_DIM
    seq = k_ref.shape[0]
    heads = range(FOX_HPS)

    @pl.when(qi == 0)
    def _():
        lane = lax.broadcasted_iota(jnp.int32, (seq, LANES), 1)
        for hh in heads:
            cols = slice(hh * hd, (hh + 1) * hd)
            c_h = jnp.sum(jnp.where(lane == hg * FOX_HPS + hh, c_ref[...], 0.0),
                          axis=1, keepdims=True)
            hi, mid, lo = (piece.astype(F32) for piece in _split3(
                jnp.broadcast_to(c_h * (1.0 / FOX_SCALE), (seq, LANES))))
            feat = jnp.where(lane == 0, hi, jnp.where(lane == 1, mid,
                                                      jnp.where(lane == 2, lo, 0.0)))
            kx_ref[hh, :, :hd] = k_ref[:, cols].astype(BF16)
            kx_ref[hh, :, hd:] = feat.astype(BF16)
            vt_ref[hh, :hd, :] = v_ref[:, cols].T.astype(BF16)
            vt_ref[hh, hd:, :] = jnp.ones((hd, seq), BF16)

    qlane = lax.broadcasted_iota(jnp.int32, (FOX_TQ, LANES), 1)
    qfeat = jnp.where(qlane < N_BIAS_FEATURES, -1.0, 0.0).astype(BF16)
    qx = [jnp.concatenate([q_ref[:, hh * hd:(hh + 1) * hd].astype(BF16), qfeat], axis=1)
          for hh in heads]

    def block(kbi, carry, masked):
        start = pl.multiple_of(kbi * FOX_TK, FOX_TK)
        out = []
        for hh in heads:
            m, acc = carry[hh]
            kx = kx_ref[hh, pl.ds(start, FOX_TK), :]
            s = lax.dot_general(kx, qx[hh], NT_DIMS, preferred_element_type=F32)
            s = s * (FOX_SCALE * LOG2E)
            if masked:
                kpos = lax.broadcasted_iota(jnp.int32, (FOX_TK, FOX_TQ), 0)
                qpos = lax.broadcasted_iota(jnp.int32, (FOX_TK, FOX_TQ), 1)
                s = jnp.where(kpos <= qpos, s, NEG_INF)
            m_new = jnp.maximum(m, jnp.max(s, axis=0, keepdims=True))
            alpha = jnp.exp2(m - m_new)
            p = jnp.exp2(s - m_new)
            acc = alpha * acc + jnp.dot(vt_ref[hh, :, pl.ds(start, FOX_TK)], p.astype(BF16),
                                        preferred_element_type=F32)
            out.append((m_new, acc))
        return tuple(out)

    init = tuple((jnp.full((1, FOX_TQ), NEG_INF, F32), jnp.zeros((2 * hd, FOX_TQ), F32))
                 for _ in heads)
    carry = lax.fori_loop(0, qi, lambda kbi, c: block(kbi, c, False), init)
    final = block(qi, carry, True)
    for hh in heads:
        acc = final[hh][1]
        o_ref[:, hh * hd:(hh + 1) * hd] = (acc[:hd] / acc[hd:]).T


def _fox_prompt(rest, k, v, c, batch, seq):
    nq = seq // FOX_TQ
    hw = FOX_HPS * FOX_HEAD_DIM
    return pl.pallas_call(
        _fox_prompt_kernel,
        grid=(batch, FOX_HEADS // FOX_HPS, nq),
        in_specs=[
            pl.BlockSpec((FOX_TQ, hw), lambda b, g, qi: (b * nq + qi, REST_QA // hw + g)),
            pl.BlockSpec((seq, hw), lambda b, g, qi: (b, g)),
            pl.BlockSpec((seq, hw), lambda b, g, qi: (b, g)),
            pl.BlockSpec((seq, LANES), lambda b, g, qi: (b, 0)),
        ],
        out_specs=pl.BlockSpec((FOX_TQ, hw), lambda b, g, qi: (b * nq + qi, g)),
        out_shape=jax.ShapeDtypeStruct((batch * seq, FOX_WIDTH), F32),
        scratch_shapes=[pltpu.VMEM((FOX_HPS, seq, 2 * FOX_HEAD_DIM), BF16),
                        pltpu.VMEM((FOX_HPS, 2 * FOX_HEAD_DIM, seq), BF16)],
        compiler_params=_cparams(("arbitrary", "arbitrary", "arbitrary")),
        name="fox_prompt",
    )(rest, k, v, c)


def _rotary(x, cos, sin):
    half = RET_HEAD_DIM // 2
    x1 = x[:, :half]
    x2 = x[:, half:]
    return jnp.concatenate([x1 * cos - x2 * sin, x2 * cos + x1 * sin], axis=-1)


def _head_rmsnorm(o, w):
    return o * lax.rsqrt(jnp.mean(o * o, axis=-1, keepdims=True) + EPS) * w


def _ret_prompt_kernel(lg_ref, q_ref, k_ref, v_ref, cos_ref, sin_ref, w_ref,
                       o_ref, st_ref):
    h = pl.program_id(1)
    lg = lg_ref[h]
    n_chunks = q_ref.shape[0] // RET_CHUNK
    ri = lax.broadcasted_iota(jnp.int32, (RET_CHUNK, RET_CHUNK), 0)
    ci = lax.broadcasted_iota(jnp.int32, (RET_CHUNK, RET_CHUNK), 1)
    diff = (ri - ci).astype(F32)
    decay = jnp.where(diff >= 0, jnp.exp(lg * jnp.maximum(diff, 0.0)), 0.0)
    pos = lax.broadcasted_iota(jnp.int32, (RET_CHUNK, RET_HEAD_DIM), 0).astype(F32)
    cross = jnp.exp(lg * (pos + 1.0))
    kdec = jnp.exp(lg * (RET_CHUNK - 1.0 - pos))
    g_chunk = jnp.exp(jnp.full((1, RET_HEAD_DIM), lg * RET_CHUNK, F32))
    w = w_ref[...]

    st_ref[0, 0] = jnp.zeros((RET_HEAD_DIM, RET_HEAD_DIM), F32)

    def chunk(c, _):
        r0 = pl.multiple_of(c * RET_CHUNK, RET_CHUNK)
        rows = pl.ds(r0, RET_CHUNK)
        cos = cos_ref[rows, :]
        sin = sin_ref[rows, :]
        qr = _rotary(q_ref[rows, :].astype(F32), cos, sin)
        kr = _rotary(k_ref[rows, :].astype(F32), cos, sin) * RET_K_SCALE
        qb = qr.astype(BF16)
        kb = kr.astype(BF16)
        vb = v_ref[rows, :].astype(BF16)
        st = st_ref[0, 0]
        scores = lax.dot_general(qb, kb, NT_DIMS, preferred_element_type=F32) * decay
        o = jnp.dot(scores.astype(BF16), vb, preferred_element_type=F32)
        o = o + cross * jnp.dot(qb, st.astype(BF16), preferred_element_type=F32)
        kd = (kr * kdec).astype(BF16)
        st_ref[0, 0] = g_chunk * st + lax.dot_general(kd, vb, TN_DIMS,
                                                      preferred_element_type=F32)
        o_ref[rows, :] = _head_rmsnorm(o, w)
        return 0

    lax.fori_loop(0, n_chunks, chunk, 0, unroll=4)


def _ret_prompt(log_gamma, rest, cos, sin, ret_norm_w, batch, seq):
    hd = RET_HEAD_DIM
    return pl.pallas_call(
        _ret_prompt_kernel,
        grid=(batch, RET_HEADS),
        in_specs=[
            pl.BlockSpec(memory_space=pltpu.SMEM),
            pl.BlockSpec((seq, hd), lambda b, h: (b, REST_QB // hd + h)),
            pl.BlockSpec((seq, hd), lambda b, h: (b, REST_KB // hd + h)),
            pl.BlockSpec((seq, hd), lambda b, h: (b, REST_VB // hd + h)),
            pl.BlockSpec((seq, hd // 2), lambda b, h: (0, 0)),
            pl.BlockSpec((seq, hd // 2), lambda b, h: (0, 0)),
            pl.BlockSpec((1, hd), lambda b, h: (0, h)),
        ],
        out_specs=[
            pl.BlockSpec((seq, hd), lambda b, h: (b, h)),
            pl.BlockSpec((1, 1, hd, hd), lambda b, h: (b, h, 0, 0)),
        ],
        out_shape=[
            jax.ShapeDtypeStruct((batch * seq, RET_WIDTH), F32),
            jax.ShapeDtypeStruct((batch, RET_HEADS, hd, hd), F32),
        ],
        compiler_params=_cparams(("arbitrary", "arbitrary")),
        name="ret_prompt",
    )(log_gamma, rest, rest, rest, cos, sin, ret_norm_w)


def _ret_sample_kernel(lg_ref, q_ref, k_ref, v_ref, cos_ref, sin_ref, w_ref, st_ref,
                       o_ref, nst_ref):
    hd = RET_HEAD_DIM
    cos = cos_ref[...]
    sin = sin_ref[...]
    first_row = lax.broadcasted_iota(jnp.int32, (16, hd), 0) == 0
    for h in range(RET_HEADS):
        cols = slice(h * hd, (h + 1) * hd)
        gamma = jnp.exp(jnp.full((1, hd), lg_ref[h], F32))
        qr = _rotary(q_ref[:, cols], cos, sin)
        kr = _rotary(k_ref[:, cols], cos, sin) * RET_K_SCALE
        vh = v_ref[:, cols]
        st = st_ref[0, h]
        q16 = jnp.broadcast_to(qr, (16, hd)).astype(BF16)
        qs = jnp.dot(q16, st.astype(BF16), preferred_element_type=F32)[0:1]
        score = jnp.sum(qr * kr, axis=-1, keepdims=True)
        o = score * vh + gamma * qs
        k16 = jnp.where(first_row, jnp.broadcast_to(kr, (16, hd)), 0.0).astype(BF16)
        v16 = jnp.broadcast_to(vh, (16, hd)).astype(BF16)
        nst_ref[0, h] = gamma * st + lax.dot_general(k16, v16, TN_DIMS,
                                                     preferred_element_type=F32)
        o_ref[:, cols] = _head_rmsnorm(o, w_ref[:, cols])


def _ret_sample(log_gamma, rest3, cos, sin, ret_norm_w, state):
    nb = rest3.shape[0]
    hd = RET_HEAD_DIM
    return pl.pallas_call(
        _ret_sample_kernel,
        grid=(nb,),
        in_specs=[
            pl.BlockSpec(memory_space=pltpu.SMEM),
            pl.BlockSpec((None, 1, RET_WIDTH), lambda b: (b, 0, REST_QB // RET_WIDTH)),
            pl.BlockSpec((None, 1, RET_WIDTH), lambda b: (b, 0, REST_KB // RET_WIDTH)),
            pl.BlockSpec((None, 1, RET_WIDTH), lambda b: (b, 0, REST_VB // RET_WIDTH)),
            pl.BlockSpec((1, hd // 2), lambda b: (0, 0)),
            pl.BlockSpec((1, hd // 2), lambda b: (0, 0)),
            pl.BlockSpec((1, RET_WIDTH), lambda b: (0, 0)),
            pl.BlockSpec((1, RET_HEADS, hd, hd), lambda b: (b, 0, 0, 0)),
        ],
        out_specs=[
            pl.BlockSpec((None, 1, RET_WIDTH), lambda b: (b, 0, 0)),
            pl.BlockSpec((1, RET_HEADS, hd, hd), lambda b: (b, 0, 0, 0)),
        ],
        out_shape=[
            jax.ShapeDtypeStruct((nb, 1, RET_WIDTH), F32),
            jax.ShapeDtypeStruct((nb, RET_HEADS, hd, hd), F32),
        ],
        compiler_params=_cparams(("arbitrary",)),
        name="ret_sample",
    )(log_gamma, rest3, rest3, rest3, cos, sin, ret_norm_w, state)


DEC_PAGES = 8
DEC_SLOTS = 2
DEC_GROUP = 2
PAIRS = PAGE_SIZE // 2
PAIR_ROWS = PAIRS * FOX_HEADS
PAIR_WIDTH = 2 * FOX_HEAD_DIM
STREAMS = 2 * FOX_HEADS
SEM_K, SEM_V, SEM_LF = 0, 1, 2


def _pair_view(page_ref):
    even = page_ref[:, 0].reshape(PAIR_ROWS, FOX_HEAD_DIM)
    odd = page_ref[:, 1].reshape(PAIR_ROWS, FOX_HEAD_DIM)
    return jnp.concatenate([even, odd], axis=1).astype(BF16)


def _fox_decode_kernel(pt_ref, q_ref, kn_ref, vn_ref, lfn_ref, ck_hbm, cv_hbm, clf_hbm,
                       o_ref, kbuf, vbuf, lfbuf, sems, m_ref, l_ref, acc_ref, carry_ref,
                       after_ref):
    np_ = DEC_PAGES
    hd = FOX_HEAD_DIM
    nh = FOX_HEADS
    b = pl.program_id(0)
    i = pl.program_id(1)
    steps_per_seq = pl.num_programs(1)
    chunks_per_seq = DEC_SLOTS * steps_per_seq
    total_chunks = pl.num_programs(0) * chunks_per_seq
    first_chunk = (b * steps_per_seq + i) * DEC_SLOTS

    def chunk_copies(chunk, slot):
        seq_id = chunk // chunks_per_seq
        first_page = (chunks_per_seq - 1 - chunk % chunks_per_seq) * np_
        copies = []
        for j in range(np_):
            page = pt_ref[seq_id, first_page + j]
            copies.append(pltpu.make_async_copy(ck_hbm.at[page], kbuf.at[slot, j],
                                                sems.at[slot, SEM_K]))
            copies.append(pltpu.make_async_copy(cv_hbm.at[page], vbuf.at[slot, j],
                                                sems.at[slot, SEM_V]))
            copies.append(pltpu.make_async_copy(clf_hbm.at[page], lfbuf.at[slot, j],
                                                sems.at[slot, SEM_LF]))
        return copies

    def start_chunk(chunk, slot):
        for cp in chunk_copies(chunk, slot):
            cp.start()

    def wait_chunk(chunk, slot):
        for cp in chunk_copies(chunk, slot):
            cp.wait()

    @pl.when((b == 0) & (i == 0))
    def _():
        prow = lax.broadcasted_iota(jnp.int32, (PAGE_SIZE, 2 * PAIR_ROWS), 0)
        lane = lax.broadcasted_iota(jnp.int32, (PAGE_SIZE, 2 * PAIR_ROWS), 1)
        pos = 2 * ((lane % PAIR_ROWS) // nh) + lane // PAIR_ROWS
        after_ref[...] = jnp.where(prow > pos, 1.0, 0.0).astype(BF16)
        for slot in range(DEC_SLOTS):
            start_chunk(slot, slot)

    @pl.when(i == 0)
    def _():
        m_ref[...] = jnp.full((STREAMS, LANES), NEG_INF, F32)
        l_ref[...] = jnp.zeros((STREAMS, LANES), F32)
        acc_ref[...] = jnp.zeros((STREAMS, PAIR_WIDTH), F32)
        carry_ref[...] = jnp.zeros((nh, LANES), F32)

    q = q_ref[...]
    qb = q.astype(BF16)
    zq = jnp.zeros((nh, hd), BF16)
    qw = jnp.concatenate([jnp.concatenate([qb, zq], axis=1),
                          jnp.concatenate([zq, qb], axis=1)], axis=0)
    srow = lax.broadcasted_iota(jnp.int32, (STREAMS, PAIR_ROWS), 0)
    slane = lax.broadcasted_iota(jnp.int32, (STREAMS, PAIR_ROWS), 1)
    own_head = (slane % nh) == (srow % nh)
    cn = lfn_ref[:, 0:1]

    def consume(slot, state):
        m, l, acc, carry = state
        lfs = lfbuf[slot].reshape(np_ * nh, PAGE_SIZE)
        after = after_ref[...]
        suffix = sum(jnp.dot(piece, after, preferred_element_type=F32)
                     for piece in _split3(lfs))
        totals = jnp.sum(lfs, axis=1, keepdims=True)
        head_bias = [None] * np_
        for j in reversed(range(np_)):
            head_bias[j] = carry + cn
            carry = carry + totals[j * nh:(j + 1) * nh]

        def logits(j):
            rows = slice(j * nh, (j + 1) * nh)
            kw = _pair_view(kbuf.at[slot, j])
            s = lax.dot_general(qw, kw, NT_DIMS, preferred_element_type=F32) * FOX_SCALE
            bias = jnp.concatenate([suffix[rows, :PAIR_ROWS] + head_bias[j],
                                    suffix[rows, PAIR_ROWS:] + head_bias[j]], axis=0)
            return jnp.where(own_head, s + bias, NEG_INF)

        def update(m, l, acc, pages, s_pages):
            m_new = m
            for s in s_pages:
                m_new = jnp.maximum(m_new, jnp.max(s, axis=1, keepdims=True))
            alpha = jnp.exp(m - m_new)
            l = alpha * l
            acc = alpha * acc
            for j, s in zip(pages, s_pages):
                p = jnp.exp(s - m_new)
                l = l + jnp.sum(p, axis=1, keepdims=True)
                acc = acc + jnp.dot(p.astype(BF16), _pair_view(vbuf.at[slot, j]),
                                    preferred_element_type=F32)
            return m_new, l, acc

        groups = [list(range(g + DEC_GROUP - 1, g - 1, -1))
                  for g in range(np_ - DEC_GROUP, -1, -DEC_GROUP)]
        s_next = [logits(j) for j in groups[0]]
        for g, pages in enumerate(groups):
            s_cur = s_next
            if g + 1 < len(groups):
                s_next = [logits(j) for j in groups[g + 1]]
            m, l, acc = update(m, l, acc, pages, s_cur)
        return m, l, acc, carry

    state = (m_ref[:, 0:1], l_ref[:, 0:1], acc_ref[...], carry_ref[:, 0:1])
    for slot in range(DEC_SLOTS):
        wait_chunk(first_chunk + slot, slot)
        state = consume(slot, state)
        refill = first_chunk + slot + DEC_SLOTS

        @pl.when(refill < total_chunks)
        def _():
            start_chunk(refill, slot)

    m, l, acc, carry = state
    m_ref[...] = jnp.broadcast_to(m, (STREAMS, LANES))
    l_ref[...] = jnp.broadcast_to(l, (STREAMS, LANES))
    acc_ref[...] = acc
    carry_ref[...] = jnp.broadcast_to(carry, (nh, LANES))

    @pl.when(i == steps_per_seq - 1)
    def _():
        sn = jnp.sum(q * kn_ref[...], axis=1, keepdims=True) * FOX_SCALE
        m_e, m_o = m[:nh], m[nh:]
        m_f = jnp.maximum(jnp.maximum(m_e, m_o), sn)
        w_e = jnp.exp(m_e - m_f)
        w_o = jnp.exp(m_o - m_f)
        p_n = jnp.exp(sn - m_f)
        denom = w_e * l[:nh] + w_o * l[nh:] + p_n
        out = w_e * acc[:nh, :hd] + w_o * acc[nh:, hd:] + p_n * vn_ref[...]
        o_ref[...] = out / denom


def _fox_decode(page_table, q3, kn3, vn3, lfn3, cache_k, cache_v, cache_lf_t):
    nb, n_pages = page_table.shape
    steps_per_seq = n_pages // (DEC_PAGES * DEC_SLOTS)
    nh, hd = FOX_HEADS, FOX_HEAD_DIM
    head_spec = pl.BlockSpec((None, nh, hd), lambda b, i, pt: (b, 0, 0))
    hbm_spec = pl.BlockSpec(memory_space=pl.ANY)
    grid_spec = pltpu.PrefetchScalarGridSpec(
        num_scalar_prefetch=1,
        grid=(nb, steps_per_seq),
        in_specs=[head_spec] * 4 + [hbm_spec] * 3,
        out_specs=head_spec,
        scratch_shapes=[
            pltpu.VMEM((DEC_SLOTS, DEC_PAGES, PAIRS, 2, nh, hd), F32),
            pltpu.VMEM((DEC_SLOTS, DEC_PAGES, PAIRS, 2, nh, hd), F32),
            pltpu.VMEM((DEC_SLOTS, DEC_PAGES, nh, PAGE_SIZE), F32),
            pltpu.SemaphoreType.DMA((DEC_SLOTS, 3)),
            pltpu.VMEM((STREAMS, LANES), F32),
            pltpu.VMEM((STREAMS, LANES), F32),
            pltpu.VMEM((STREAMS, PAIR_WIDTH), F32),
            pltpu.VMEM((nh, LANES), F32),
            pltpu.VMEM((PAGE_SIZE, 2 * PAIR_ROWS), BF16),
        ],
    )
    return pl.pallas_call(
        _fox_decode_kernel,
        grid_spec=grid_spec,
        out_shape=jax.ShapeDtypeStruct((nb, nh, hd), F32),
        compiler_params=_cparams(("arbitrary", "arbitrary")),
        name="fox_decode",
    )(page_table, q3, kn3, vn3, lfn3, cache_k, cache_v, cache_lf_t)


def _merge_kernel(x_ref, oa_ref, za_ref, ob_ref, zb_ref, ga_ref, gb_ref,
                  wua_ref, wub_ref, wo_ref, fnw_ref, y_ref):
    a = (oa_ref[...] * jax.nn.silu(za_ref[...].astype(F32))).astype(BF16)
    b = (ob_ref[...] * jax.nn.silu(zb_ref[...].astype(F32))).astype(BF16)
    ua = jnp.dot(a, wua_ref[...], preferred_element_type=F32)
    ub = jnp.dot(b, wub_ref[...], preferred_element_type=F32)
    merged = (jax.nn.sigmoid(ga_ref[...].astype(F32)) * ua
              + jax.nn.sigmoid(gb_ref[...].astype(F32)) * ub)
    out = x_ref[...] + jnp.dot(merged.astype(BF16), wo_ref[...], preferred_element_type=F32)
    ms = jnp.mean(out * out, axis=-1, keepdims=True)
    y_ref[...] = out * lax.rsqrt(ms + EPS) * fnw_ref[...]


def _merge(x, o_a, o_b, rest, w_up_a, w_up_b, w_o, final_norm_w, tm):
    m = x.shape[0]
    const = dict(pipeline_mode=pl.Buffered(1))
    return pl.pallas_call(
        _merge_kernel,
        grid=(m // tm,),
        in_specs=[
            pl.BlockSpec((tm, D_MODEL), lambda i: (i, 0)),
            pl.BlockSpec((tm, FOX_WIDTH), lambda i: (i, 0)),
            pl.BlockSpec((tm, FOX_WIDTH), lambda i: (i, REST_ZA // FOX_WIDTH)),
            pl.BlockSpec((tm, RET_WIDTH), lambda i: (i, 0)),
            pl.BlockSpec((tm, RET_WIDTH), lambda i: (i, REST_ZB // RET_WIDTH)),
            pl.BlockSpec((tm, D_MODEL), lambda i: (i, REST_GA // D_MODEL)),
            pl.BlockSpec((tm, D_MODEL), lambda i: (i, REST_GB // D_MODEL)),
            pl.BlockSpec((FOX_WIDTH, D_MODEL), lambda i: (0, 0), **const),
            pl.BlockSpec((RET_WIDTH, D_MODEL), lambda i: (0, 0), **const),
            pl.BlockSpec((D_MODEL, D_MODEL), lambda i: (0, 0), **const),
            pl.BlockSpec((1, D_MODEL), lambda i: (0, 0)),
        ],
        out_specs=pl.BlockSpec((tm, D_MODEL), lambda i: (i, 0)),
        out_shape=jax.ShapeDtypeStruct((m, D_MODEL), F32),
        compiler_params=_cparams(("arbitrary",)),
        name="merge",
    )(x, o_a, rest, o_b, rest, rest, rest, w_up_a, w_up_b, w_o, final_norm_w)


def _rope_tables(pos):
    half = RET_HEAD_DIM // 2
    inv = 1.0 / (ROPE_BASE ** (jnp.arange(half, dtype=F32) / half))
    ang = pos.astype(F32)[:, None] * inv[None, :]
    return jnp.cos(ang), jnp.sin(ang)


def _layer(xp, xs, cache_k, cache_v, cache_logf, state_ret, page_table,
           norm_w, w_in, b_f, ret_norm_w, w_up_a, w_up_b, w_o, final_norm_w):
    batch, seq, _ = xp.shape
    nb = xs.shape[0]
    n_pool = cache_k.shape[0]
    past = page_table.shape[1] * PAGE_SIZE
    nh, hd, fw, rw = FOX_HEADS, FOX_HEAD_DIM, FOX_WIDTH, RET_WIDTH

    wt = w_in.T
    bf_pad = jnp.pad(b_f, (0, F_ROWS - nh)).reshape(1, F_ROWS)
    wua, wub, wo = w_up_a.astype(BF16), w_up_b.astype(BF16), w_o.astype(BF16)
    nw = norm_w.reshape(1, D_MODEL)
    rnw = ret_norm_w.reshape(1, rw)
    fnw = final_norm_w.reshape(1, D_MODEL)
    log_gamma = jnp.log(1.0 - 2.0 ** (-5.0 - jnp.arange(RET_HEADS, dtype=F32)))
    cos_p, sin_p = _rope_tables(jnp.arange(seq, dtype=jnp.int32))
    cos_s, sin_s = _rope_tables(past + jnp.arange(1, dtype=jnp.int32))

    x2 = xp.reshape(batch * seq, D_MODEL)
    k_p, v_p, rest_p, f_p = _inproj(x2, nw, wt, tm=1024, rest_dtype=BF16)
    logf_p, c_p = _logf_cumsum(f_p, bf_pad, batch, seq)
    o_a = _fox_prompt(rest_p, k_p, v_p, c_p, batch, seq)
    o_b, st_p = _ret_prompt(log_gamma, rest_p, cos_p, sin_p, rnw, batch, seq)
    y_p = _merge(x2, o_a, o_b, rest_p, wua, wub, wo, fnw, tm=256)

    xs2 = xs.reshape(nb, D_MODEL)
    k_s, v_s, rest_s, f_s = _inproj(xs2, nw, wt, tm=nb, rest_dtype=F32)
    logf_s = _logf(f_s, bf_pad)
    rest_s3 = rest_s.reshape(nb, 1, REST_WIDTH)
    q_s = rest_s[:, REST_QA:REST_QA + fw].reshape(nb, nh, hd)
    lfn = jnp.broadcast_to(logf_s[:, :, None], (nb, nh, hd))
    o_a_s = _fox_decode(
        page_table, q_s, k_s.reshape(nb, nh, hd), v_s.reshape(nb, nh, hd), lfn,
        cache_k.reshape(n_pool, PAIRS, 2, nh, hd), cache_v.reshape(n_pool, PAIRS, 2, nh, hd),
        jnp.swapaxes(cache_logf, 1, 2))
    o_b_s, st_s = _ret_sample(log_gamma, rest_s3, cos_s, sin_s, rnw, state_ret)
    y_s = _merge(xs2, o_a_s.reshape(nb, fw), o_b_s.reshape(nb, rw), rest_s,
                 wua, wub, wo, fnw, tm=nb)

    hshape = (nh, hd)
    return (y_p.reshape(batch, seq, D_MODEL), y_s.reshape(nb, 1, D_MODEL),
            k_p.reshape(batch, seq, *hshape), v_p.reshape(batch, seq, *hshape),
            logf_p.reshape(batch, seq, nh), st_p,
            k_s.reshape(nb, 1, *hshape), v_s.reshape(nb, 1, *hshape),
            logf_s.reshape(nb, 1, nh), st_s)


def kernel(x_prompt, x_sample, cache_k, cache_v, cache_logf, state_ret, page_table,
           norm_w, w_in, b_f, ret_norm_w, w_up_a, w_up_b, w_o, final_norm_w):
    depth = w_in.shape[0]
    assert depth == 1 and x_sample.shape[1] == 1
    outs = _layer(x_prompt, x_sample, cache_k[0], cache_v[0], cache_logf[0], state_ret[0],
                  page_table, norm_w[0], w_in[0], b_f[0], ret_norm_w[0], w_up_a[0],
                  w_up_b[0], w_o[0], final_norm_w)
    y_p, y_s, k_p, v_p, lf_p, st_p, k_s, v_s, lf_s, st_s = outs
    stack = lambda a: a[None]
    return (y_p, y_s, stack(k_p), stack(v_p), stack(lf_p), stack(st_p),
            stack(k_s), stack(v_s), stack(lf_s), stack(st_s))
```

```python
import functools
import math

import jax
import jax.numpy as jnp
from jax import lax
from jax.experimental import pallas as pl
from jax.experimental.pallas import tpu as pltpu

F32 = jnp.float32
BF16 = jnp.bfloat16

D_MODEL = 2048
FOX_HEADS = 8
FOX_HEAD_DIM = 128
FOX_WIDTH = FOX_HEADS * FOX_HEAD_DIM
RET_HEADS = 4
RET_HEAD_DIM = 256
RET_WIDTH = RET_HEADS * RET_HEAD_DIM
RET_CHUNK = 128
PAGE_SIZE = 128
ROPE_BASE = 10000.0
EPS = 1e-6
NEG_INF = -1e30
FOX_SCALE = FOX_HEAD_DIM ** -0.5
RET_K_SCALE = RET_HEAD_DIM ** -0.5
LOG2E = math.log2(math.e)

LANES = 128
SUBLANES = 8
VMEM_LIMIT = 56 * 1024 * 1024

REST_QA, REST_ZA, REST_QB, REST_KB, REST_VB, REST_ZB, REST_GA, REST_GB = (
    0, 1024, 2048, 3072, 4096, 5120, 6144, 8192)
REST_WIDTH = 10240
PROJ_TN = 512
N_KV_TILES = FOX_WIDTH // PROJ_TN
F_ROWS = LANES
W_STREAMS = 4

NT_DIMS = (((1,), (1,)), ((), ()))
TN_DIMS = (((0,), (0,)), ((), ()))


def _cparams(sem, vmem=VMEM_LIMIT):
    return pltpu.CompilerParams(dimension_semantics=sem, vmem_limit_bytes=vmem)


def _inproj_kernel(x_ref, nw_ref, *refs, weights_as_lhs):
    wt_refs = refs[:W_STREAMS]
    wf_ref, k_ref, v_ref, rest_ref, f_ref, h_ref = refs[W_STREAMS:]
    j = pl.program_id(1)
    m = x_ref.shape[0]

    def project(w_refs):
        w = jnp.concatenate([r[...] for r in w_refs], axis=0).astype(BF16)
        if weights_as_lhs:
            out_t = lax.dot_general(w, h_ref[...], NT_DIMS, preferred_element_type=F32)
            return out_t.T[:m]
        return lax.dot_general(h_ref[...], w, NT_DIMS, preferred_element_type=F32)

    @pl.when(j == 0)
    def _():
        x = x_ref[...]
        ms = jnp.mean(x * x, axis=-1, keepdims=True)
        h = (x * lax.rsqrt(ms + EPS) * nw_ref[...]).astype(BF16)
        if h_ref.shape[0] != m:
            h_ref[...] = jnp.zeros(h_ref.shape, BF16)
        h_ref[:m, :] = h
        f_ref[...] = project([wf_ref])

    acc = project(wt_refs)

    @pl.when(j < N_KV_TILES)
    def _():
        k_ref[...] = acc

    @pl.when((j >= N_KV_TILES) & (j < 2 * N_KV_TILES))
    def _():
        v_ref[...] = acc

    @pl.when(j >= 2 * N_KV_TILES)
    def _():
        rest_ref[...] = acc.astype(rest_ref.dtype)


def _wt_row_offset(j):
    t, n, fw = PROJ_TN, N_KV_TILES, FOX_WIDTH
    after_f = 4 * fw + FOX_HEADS
    return jnp.where(
        j < n, fw + j * t,
        jnp.where(j < 2 * n, 2 * fw + (j - n) * t,
                  jnp.where(j < 3 * n, (j - 2 * n) * t,
                            jnp.where(j < 4 * n, 3 * fw + (j - 3 * n) * t,
                                      after_f + (j - 4 * n) * t))))


def _inproj(x, norm_w, wt, tm, rest_dtype):
    m = x.shape[0]
    n_tiles = (wt.shape[0] - FOX_HEADS) // PROJ_TN
    kv_last = N_KV_TILES - 1
    weights_as_lhs = tm < LANES
    rows = PROJ_TN // W_STREAMS

    def wt_map(s, i, j):
        return (pl.multiple_of(_wt_row_offset(j) + s * rows, SUBLANES), 0)

    return pl.pallas_call(
        functools.partial(_inproj_kernel, weights_as_lhs=weights_as_lhs),
        grid=(m // tm, n_tiles),
        in_specs=[
            pl.BlockSpec((tm, D_MODEL), lambda i, j: (i, 0)),
            pl.BlockSpec((1, D_MODEL), lambda i, j: (0, 0)),
            *[pl.BlockSpec((pl.Element(rows), pl.Element(D_MODEL)), functools.partial(wt_map, s))
              for s in range(W_STREAMS)],
            pl.BlockSpec((F_ROWS, D_MODEL), lambda i, j: (4 * FOX_WIDTH // F_ROWS, 0)),
        ],
        out_specs=[
            pl.BlockSpec((tm, PROJ_TN), lambda i, j: (i, jnp.minimum(j, kv_last))),
            pl.BlockSpec((tm, PROJ_TN),
                         lambda i, j: (i, jnp.clip(j - N_KV_TILES, 0, kv_last))),
            pl.BlockSpec((tm, PROJ_TN),
                         lambda i, j: (i, jnp.maximum(j - 2 * N_KV_TILES, 0))),
            pl.BlockSpec((tm, F_ROWS), lambda i, j: (i, 0)),
        ],
        out_shape=[
            jax.ShapeDtypeStruct((m, FOX_WIDTH), F32),
            jax.ShapeDtypeStruct((m, FOX_WIDTH), F32),
            jax.ShapeDtypeStruct((m, REST_WIDTH), rest_dtype),
            jax.ShapeDtypeStruct((m, F_ROWS), F32),
        ],
        scratch_shapes=[pltpu.VMEM((max(tm, LANES), D_MODEL), BF16)],
        compiler_params=_cparams(("arbitrary", "arbitrary")),
        name="inproj",
    )(x, norm_w, *([wt] * W_STREAMS), wt)


def _logf_kernel(f_ref, bf_ref, logf_ref):
    lf = jax.nn.log_sigmoid(f_ref[...] + bf_ref[...])
    logf_ref[...] = lf[:, :FOX_HEADS]


def _logf(f, bf_pad):
    m = f.shape[0]
    return pl.pallas_call(
        _logf_kernel,
        grid=(1,),
        in_specs=[pl.BlockSpec((m, F_ROWS), lambda i: (0, 0)),
                  pl.BlockSpec((1, F_ROWS), lambda i: (0, 0))],
        out_specs=pl.BlockSpec((m, FOX_HEADS), lambda i: (0, 0)),
        out_shape=jax.ShapeDtypeStruct((m, FOX_HEADS), F32),
        name="logf_sample",
    )(f, bf_pad)


def _logf_cumsum_kernel(f_ref, bf_ref, logf_ref, c_ref):
    seq = f_ref.shape[0]
    lf = jax.nn.log_sigmoid(f_ref[...] + bf_ref[...])
    logf_ref[...] = lf[:, :FOX_HEADS]
    row = lax.broadcasted_iota(jnp.int32, (LANES, LANES), 0)
    col = lax.broadcasted_iota(jnp.int32, (LANES, LANES), 1)
    lower = (col <= row).astype(F32)
    carry = jnp.zeros((1, LANES), F32)
    for blk in range(seq // LANES):
        rows = slice(blk * LANES, (blk + 1) * LANES)
        cs = jnp.dot(lower, lf[rows, :], preferred_element_type=F32,
                     precision=lax.Precision.HIGHEST) + carry
        c_ref[rows, :] = cs
        carry = cs[LANES - 1:LANES, :]


def _logf_cumsum(f, bf_pad, batch, seq):
    return pl.pallas_call(
        _logf_cumsum_kernel,
        grid=(batch,),
        in_specs=[pl.BlockSpec((seq, F_ROWS), lambda b: (b, 0)),
                  pl.BlockSpec((1, F_ROWS), lambda b: (0, 0))],
        out_specs=[pl.BlockSpec((seq, FOX_HEADS), lambda b: (b, 0)),
                   pl.BlockSpec((seq, LANES), lambda b: (b, 0))],
        out_shape=[jax.ShapeDtypeStruct((batch * seq, FOX_HEADS), F32),
                   jax.ShapeDtypeStruct((batch * seq, LANES), F32)],
        compiler_params=_cparams(("arbitrary",)),
        name="logf_cumsum",
    )(f, bf_pad)


FOX_TQ = 512
FOX_TK = 512
FOX_HPS = 2
N_BIAS_FEATURES = 3


def _split3(x):
    hi = x.astype(BF16)
    r1 = x - hi.astype(F32)
    mid = r1.astype(BF16)
    lo = (r1 - mid.astype(F32)).astype(BF16)
    return hi, mid, lo


def _fox_prompt_kernel(q_ref, k_ref, v_ref, c_ref, o_ref, kx_ref, vt_ref):
    hg = pl.program_id(1)
    qi = pl.program_id(2)
    hd = FOX_HEAD_DIM
    seq = k_ref.shape[0]
    heads = range(FOX_HPS)

    @pl.when(qi == 0)
    def _():
        lane = lax.broadcasted_iota(jnp.int32, (seq, LANES), 1)
        for hh in heads:
            cols = slice(hh * hd, (hh + 1) * hd)
            c_h = jnp.sum(jnp.where(lane == hg * FOX_HPS + hh, c_ref[...], 0.0),
                          axis=1, keepdims=True)
            hi, mid, lo = (piece.astype(F32) for piece in _split3(
                jnp.broadcast_to(c_h * (1.0 / FOX_SCALE), (seq, LANES))))
            feat = jnp.where(lane == 0, hi, jnp.where(lane == 1, mid,
                                                      jnp.where(lane == 2, lo, 0.0)))
            kx_ref[hh, :, :hd] = k_ref[:, cols].astype(BF16)
            kx_ref[hh, :, hd:] = feat.astype(BF16)
            vt_ref[hh, :hd, :] = v_ref[:, cols].T.astype(BF16)
            vt_ref[hh, hd:, :] = jnp.ones((hd, seq), BF16)

    qlane = lax.broadcasted_iota(jnp.int32, (FOX_TQ, LANES), 1)
    qfeat = jnp.where(qlane < N_BIAS_FEATURES, -1.0, 0.0).astype(BF16)
    qx = [jnp.concatenate([q_ref[:, hh * hd:(hh + 1) * hd].astype(BF16), qfeat], axis=1)
          for hh in heads]

    def block(kbi, carry, masked):
        start = pl.multiple_of(kbi * FOX_TK, FOX_TK)
        out = []
        for hh in heads:
            m, acc = carry[hh]
            kx = kx_ref[hh, pl.ds(start, FOX_TK), :]
            s = lax.dot_general(kx, qx[hh], NT_DIMS, preferred_element_type=F32)
            s = s * (FOX_SCALE * LOG2E)
            if masked:
                kpos = lax.broadcasted_iota(jnp.int32, (FOX_TK, FOX_TQ), 0)
                qpos = lax.broadcasted_iota(jnp.int32, (FOX_TK, FOX_TQ), 1)
                s = jnp.where(kpos <= qpos, s, NEG_INF)
            m_new = jnp.maximum(m, jnp.max(s, axis=0, keepdims=True))
            alpha = jnp.exp2(m - m_new)
            p = jnp.exp2(s - m_new)
            acc = alpha * acc + jnp.dot(vt_ref[hh, :, pl.ds(start, FOX_TK)], p.astype(BF16),
                                        preferred_element_type=F32)
            out.append((m_new, acc))
        return tuple(out)

    init = tuple((jnp.full((1, FOX_TQ), NEG_INF, F32), jnp.zeros((2 * hd, FOX_TQ), F32))
                 for _ in heads)
    carry = lax.fori_loop(0, qi, lambda kbi, c: block(kbi, c, False), init)
    final = block(qi, carry, True)
    for hh in heads:
        acc = final[hh][1]
        o_ref[:, hh * hd:(hh + 1) * hd] = (acc[:hd] / acc[hd:]).T


def _fox_prompt(rest, k, v, c, batch, seq):
    nq = seq // FOX_TQ
    hw = FOX_HPS * FOX_HEAD_DIM
    return pl.pallas_call(
        _fox_prompt_kernel,
        grid=(batch, FOX_HEADS // FOX_HPS, nq),
        in_specs=[
            pl.BlockSpec((FOX_TQ, hw), lambda b, g, qi: (b * nq + qi, REST_QA // hw + g)),
            pl.BlockSpec((seq, hw), lambda b, g, qi: (b, g)),
            pl.BlockSpec((seq, hw), lambda b, g, qi: (b, g)),
            pl.BlockSpec((seq, LANES), lambda b, g, qi: (b, 0)),
        ],
        out_specs=pl.BlockSpec((FOX_TQ, hw), lambda b, g, qi: (b * nq + qi, g)),
        out_shape=jax.ShapeDtypeStruct((batch * seq, FOX_WIDTH), F32),
        scratch_shapes=[pltpu.VMEM((FOX_HPS, seq, 2 * FOX_HEAD_DIM), BF16),
                        pltpu.VMEM((FOX_HPS, 2 * FOX_HEAD_DIM, seq), BF16)],
        compiler_params=_cparams(("arbitrary", "arbitrary", "arbitrary")),
        name="fox_prompt",
    )(rest, k, v, c)


def _rotary(x, cos, sin):
    half = RET_HEAD_DIM // 2
    x1 = x[:, :half]
    x2 = x[:, half:]
    return jnp.concatenate([x1 * cos - x2 * sin, x2 * cos + x1 * sin], axis=-1)


def _head_rmsnorm(o, w):
    return o * lax.rsqrt(jnp.mean(o * o, axis=-1, keepdims=True) + EPS) * w


def _ret_prompt_kernel(lg_ref, q_ref, k_ref, v_ref, cos_ref, sin_ref, w_ref,
                       o_ref, st_ref):
    h = pl.program_id(1)
    lg = lg_ref[h]
    n_chunks = q_ref.shape[0] // RET_CHUNK
    ri = lax.broadcasted_iota(jnp.int32, (RET_CHUNK, RET_CHUNK), 0)
    ci = lax.broadcasted_iota(jnp.int32, (RET_CHUNK, RET_CHUNK), 1)
    diff = (ri - ci).astype(F32)
    decay = jnp.where(diff >= 0, jnp.exp(lg * jnp.maximum(diff, 0.0)), 0.0)
    pos = lax.broadcasted_iota(jnp.int32, (RET_CHUNK, RET_HEAD_DIM), 0).astype(F32)
    cross = jnp.exp(lg * (pos + 1.0))
    kdec = jnp.exp(lg * (RET_CHUNK - 1.0 - pos))
    g_chunk = jnp.exp(jnp.full((1, RET_HEAD_DIM), lg * RET_CHUNK, F32))
    w = w_ref[...]

    st_ref[0, 0] = jnp.zeros((RET_HEAD_DIM, RET_HEAD_DIM), F32)

    def chunk(c, _):
        r0 = pl.multiple_of(c * RET_CHUNK, RET_CHUNK)
        rows = pl.ds(r0, RET_CHUNK)
        cos = cos_ref[rows, :]
        sin = sin_ref[rows, :]
        qr = _rotary(q_ref[rows, :].astype(F32), cos, sin)
        kr = _rotary(k_ref[rows, :].astype(F32), cos, sin) * RET_K_SCALE
        qb = qr.astype(BF16)
        kb = kr.astype(BF16)
        vb = v_ref[rows, :].astype(BF16)
        st = st_ref[0, 0]
        scores = lax.dot_general(qb, kb, NT_DIMS, preferred_element_type=F32) * decay
        o = jnp.dot(scores.astype(BF16), vb, preferred_element_type=F32)
        o = o + cross * jnp.dot(qb, st.astype(BF16), preferred_element_type=F32)
        kd = (kr * kdec).astype(BF16)
        st_ref[0, 0] = g_chunk * st + lax.dot_general(kd, vb, TN_DIMS,
                                                      preferred_element_type=F32)
        o_ref[rows, :] = _head_rmsnorm(o, w)
        return 0

    lax.fori_loop(0, n_chunks, chunk, 0, unroll=4)


def _ret_prompt(log_gamma, rest, cos, sin, ret_norm_w, batch, seq):
    hd = RET_HEAD_DIM
    return pl.pallas_call(
        _ret_prompt_kernel,
        grid=(batch, RET_HEADS),
        in_specs=[
            pl.BlockSpec(memory_space=pltpu.SMEM),
            pl.BlockSpec((seq, hd), lambda b, h: (b, REST_QB // hd + h)),
            pl.BlockSpec((seq, hd), lambda b, h: (b, REST_KB // hd + h)),
            pl.BlockSpec((seq, hd), lambda b, h: (b, REST_VB // hd + h)),
            pl.BlockSpec((seq, hd // 2), lambda b, h: (0, 0)),
            pl.BlockSpec((seq, hd // 2), lambda b, h: (0, 0)),
            pl.BlockSpec((1, hd), lambda b, h: (0, h)),
        ],
        out_specs=[
            pl.BlockSpec((seq, hd), lambda b, h: (b, h)),
            pl.BlockSpec((1, 1, hd, hd), lambda b, h: (b, h, 0, 0)),
        ],
        out_shape=[
            jax.ShapeDtypeStruct((batch * seq, RET_WIDTH), F32),
            jax.ShapeDtypeStruct((batch, RET_HEADS, hd, hd), F32),
        ],
        compiler_params=_cparams(("arbitrary", "arbitrary")),
        name="ret_prompt",
    )(log_gamma, rest, rest, rest, cos, sin, ret_norm_w)


def _ret_sample_kernel(lg_ref, q_ref, k_ref, v_ref, cos_ref, sin_ref, w_ref, st_ref,
                       o_ref, nst_ref):
    hd = RET_HEAD_DIM
    cos = cos_ref[...]
    sin = sin_ref[...]
    first_row = lax.broadcasted_iota(jnp.int32, (16, hd), 0) == 0
    for h in range(RET_HEADS):
        cols = slice(h * hd, (h + 1) * hd)
        gamma = jnp.exp(jnp.full((1, hd), lg_ref[h], F32))
        qr = _rotary(q_ref[:, cols], cos, sin)
        kr = _rotary(k_ref[:, cols], cos, sin) * RET_K_SCALE
        vh = v_ref[:, cols]
        st = st_ref[0, h]
        q16 = jnp.broadcast_to(qr, (16, hd)).astype(BF16)
        qs = jnp.dot(q16, st.astype(BF16), preferred_element_type=F32)[0:1]
        score = jnp.sum(qr * kr, axis=-1, keepdims=True)
        o = score * vh + gamma * qs
        k16 = jnp.where(first_row, jnp.broadcast_to(kr, (16, hd)), 0.0).astype(BF16)
        v16 = jnp.broadcast_to(vh, (16, hd)).astype(BF16)
        nst_ref[0, h] = gamma * st + lax.dot_general(k16, v16, TN_DIMS,
                                                     preferred_element_type=F32)
        o_ref[:, cols] = _head_rmsnorm(o, w_ref[:, cols])


def _ret_sample(log_gamma, rest3, cos, sin, ret_norm_w, state):
    nb = rest3.shape[0]
    hd = RET_HEAD_DIM
    return pl.pallas_call(
        _ret_sample_kernel,
        grid=(nb,),
        in_specs=[
            pl.BlockSpec(memory_space=pltpu.SMEM),
            pl.BlockSpec((None, 1, RET_WIDTH), lambda b: (b, 0, REST_QB // RET_WIDTH)),
            pl.BlockSpec((None, 1, RET_WIDTH), lambda b: (b, 0, REST_KB // RET_WIDTH)),
            pl.BlockSpec((None, 1, RET_WIDTH), lambda b: (b, 0, REST_VB // RET_WIDTH)),
            pl.BlockSpec((1, hd // 2), lambda b: (0, 0)),
            pl.BlockSpec((1, hd // 2), lambda b: (0, 0)),
            pl.BlockSpec((1, RET_WIDTH), lambda b: (0, 0)),
            pl.BlockSpec((1, RET_HEADS, hd, hd), lambda b: (b, 0, 0, 0)),
        ],
        out_specs=[
            pl.BlockSpec((None, 1, RET_WIDTH), lambda b: (b, 0, 0)),
            pl.BlockSpec((1, RET_HEADS, hd, hd), lambda b: (b, 0, 0, 0)),
        ],
        out_shape=[
            jax.ShapeDtypeStruct((nb, 1, RET_WIDTH), F32),
            jax.ShapeDtypeStruct((nb, RET_HEADS, hd, hd), F32),
        ],
        compiler_params=_cparams(("arbitrary",)),
        name="ret_sample",
    )(log_gamma, rest3, rest3, rest3, cos, sin, ret_norm_w, state)


DEC_PAGES = 8
DEC_SLOTS = 4
DEC_GROUP = 2
PAIRS = PAGE_SIZE // 2
PAIR_ROWS = PAIRS * FOX_HEADS
PAIR_WIDTH = 2 * FOX_HEAD_DIM
STREAMS = 2 * FOX_HEADS
SEM_K, SEM_V, SEM_LF = 0, 1, 2


def _pair_view(page_ref):
    even = page_ref[:, 0].reshape(PAIR_ROWS, FOX_HEAD_DIM)
    odd = page_ref[:, 1].reshape(PAIR_ROWS, FOX_HEAD_DIM)
    return jnp.concatenate([even, odd], axis=1).astype(BF16)


def _fox_decode_kernel(pt_ref, q_ref, kn_ref, vn_ref, lfn_ref, ck_hbm, cv_hbm, clf_hbm,
                       o_ref, kbuf, vbuf, lfbuf, sems, m_ref, l_ref, acc_ref, carry_ref,
                       after_ref):
    np_ = DEC_PAGES
    hd = FOX_HEAD_DIM
    nh = FOX_HEADS
    b = pl.program_id(0)
    i = pl.program_id(1)
    steps_per_seq = pl.num_programs(1)
    chunks_per_seq = DEC_SLOTS * steps_per_seq
    total_chunks = pl.num_programs(0) * chunks_per_seq
    first_chunk = (b * steps_per_seq + i) * DEC_SLOTS

    def chunk_copies(chunk, slot):
        seq_id = chunk // chunks_per_seq
        first_page = (chunks_per_seq - 1 - chunk % chunks_per_seq) * np_
        copies = []
        for j in range(np_):
            page = pt_ref[seq_id, first_page + j]
            copies.append(pltpu.make_async_copy(ck_hbm.at[page], kbuf.at[slot, j],
                                                sems.at[slot, SEM_K]))
            copies.append(pltpu.make_async_copy(cv_hbm.at[page], vbuf.at[slot, j],
                                                sems.at[slot, SEM_V]))
            copies.append(pltpu.make_async_copy(clf_hbm.at[page], lfbuf.at[slot, j],
                                                sems.at[slot, SEM_LF]))
        return copies

    def start_chunk(chunk, slot):
        for cp in chunk_copies(chunk, slot):
            cp.start()

    def wait_chunk(chunk, slot):
        for cp in chunk_copies(chunk, slot):
            cp.wait()

    @pl.when((b == 0) & (i == 0))
    def _():
        prow = lax.broadcasted_iota(jnp.int32, (PAGE_SIZE, 2 * PAIR_ROWS), 0)
        lane = lax.broadcasted_iota(jnp.int32, (PAGE_SIZE, 2 * PAIR_ROWS), 1)
        pos = 2 * ((lane % PAIR_ROWS) // nh) + lane // PAIR_ROWS
        after_ref[...] = jnp.where(prow > pos, 1.0, 0.0).astype(BF16)
        for slot in range(DEC_SLOTS):
            start_chunk(slot, slot)

    @pl.when(i == 0)
    def _():
        m_ref[...] = jnp.full((STREAMS, LANES), NEG_INF, F32)
        l_ref[...] = jnp.zeros((STREAMS, LANES), F32)
        acc_ref[...] = jnp.zeros((STREAMS, PAIR_WIDTH), F32)
        carry_ref[...] = jnp.zeros((nh, LANES), F32)

    q = q_ref[...]
    qb = q.astype(BF16)
    zq = jnp.zeros((nh, hd), BF16)
    qw = jnp.concatenate([jnp.concatenate([qb, zq], axis=1),
                          jnp.concatenate([zq, qb], axis=1)], axis=0)
    srow = lax.broadcasted_iota(jnp.int32, (STREAMS, PAIR_ROWS), 0)
    slane = lax.broadcasted_iota(jnp.int32, (STREAMS, PAIR_ROWS), 1)
    own_head = (slane % nh) == (srow % nh)
    cn = lfn_ref[:, 0:1]

    def consume(slot, state):
        m, l, acc, carry = state
        lfs = lfbuf[slot].reshape(np_ * nh, PAGE_SIZE)
        after = after_ref[...]
        suffix = sum(jnp.dot(piece, after, preferred_element_type=F32)
                     for piece in _split3(lfs))
        totals = jnp.sum(lfs, axis=1, keepdims=True)
        head_bias = [None] * np_
        for j in reversed(range(np_)):
            head_bias[j] = carry + cn
            carry = carry + totals[j * nh:(j + 1) * nh]

        def logits(j):
            rows = slice(j * nh, (j + 1) * nh)
            kw = _pair_view(kbuf.at[slot, j])
            s = lax.dot_general(qw, kw, NT_DIMS, preferred_element_type=F32) * FOX_SCALE
            bias = jnp.concatenate([suffix[rows, :PAIR_ROWS] + head_bias[j],
                                    suffix[rows, PAIR_ROWS:] + head_bias[j]], axis=0)
            return jnp.where(own_head, s + bias, NEG_INF)

        def update(m, l, acc, pages, s_pages):
            m_new = m
            for s in s_pages:
                m_new = jnp.maximum(m_new, jnp.max(s, axis=1, keepdims=True))
            alpha = jnp.exp(m - m_new)
            l = alpha * l
            acc = alpha * acc
            for j, s in zip(pages, s_pages):
                p = jnp.exp(s - m_new)
                l = l + jnp.sum(p, axis=1, keepdims=True)
                acc = acc + jnp.dot(p.astype(BF16), _pair_view(vbuf.at[slot, j]),
                                    preferred_element_type=F32)
            return m_new, l, acc

        groups = [list(range(g + DEC_GROUP - 1, g - 1, -1))
                  for g in range(np_ - DEC_GROUP, -1, -DEC_GROUP)]
        s_next = [logits(j) for j in groups[0]]
        for g, pages in enumerate(groups):
            s_cur = s_next
            if g + 1 < len(groups):
                s_next = [logits(j) for j in groups[g + 1]]
            m, l, acc = update(m, l, acc, pages, s_cur)
        return m, l, acc, carry

    state = (m_ref[:, 0:1], l_ref[:, 0:1], acc_ref[...], carry_ref[:, 0:1])
    for slot in range(DEC_SLOTS):
        wait_chunk(first_chunk + slot, slot)
        state = consume(slot, state)
        refill = first_chunk + slot + DEC_SLOTS

        @pl.when(refill < total_chunks)
        def _():
            start_chunk(refill, slot)

    m, l, acc, carry = state
    m_ref[...] = jnp.broadcast_to(m, (STREAMS, LANES))
    l_ref[...] = jnp.broadcast_to(l, (STREAMS, LANES))
    acc_ref[...] = acc
    carry_ref[...] = jnp.broadcast_to(carry, (nh, LANES))

    @pl.when(i == steps_per_seq - 1)
    def _():
        sn = jnp.sum(q * kn_ref[...], axis=1, keepdims=True) * FOX_SCALE
        m_e, m_o = m[:nh], m[nh:]
        m_f = jnp.maximum(jnp.maximum(m_e, m_o), sn)
        w_e = jnp.exp(m_e - m_f)
        w_o = jnp.exp(m_o - m_f)
        p_n = jnp.exp(sn - m_f)
        denom = w_e * l[:nh] + w_o * l[nh:] + p_n
        out = w_e * acc[:nh, :hd] + w_o * acc[nh:, hd:] + p_n * vn_ref[...]
        o_ref[...] = out / denom


def _fox_decode(page_table, q3, kn3, vn3, lfn3, cache_k, cache_v, cache_lf_t):
    nb, n_pages = page_table.shape
    steps_per_seq = n_pages // (DEC_PAGES * DEC_SLOTS)
    nh, hd = FOX_HEADS, FOX_HEAD_DIM
    head_spec = pl.BlockSpec((None, nh, hd), lambda b, i, pt: (b, 0, 0))
    hbm_spec = pl.BlockSpec(memory_space=pl.ANY)
    grid_spec = pltpu.PrefetchScalarGridSpec(
        num_scalar_prefetch=1,
        grid=(nb, steps_per_seq),
        in_specs=[head_spec] * 4 + [hbm_spec] * 3,
        out_specs=head_spec,
        scratch_shapes=[
            pltpu.VMEM((DEC_SLOTS, DEC_PAGES, PAIRS, 2, nh, hd), F32),
            pltpu.VMEM((DEC_SLOTS, DEC_PAGES, PAIRS, 2, nh, hd), F32),
            pltpu.VMEM((DEC_SLOTS, DEC_PAGES, nh, PAGE_SIZE), F32),
            pltpu.SemaphoreType.DMA((DEC_SLOTS, 3)),
            pltpu.VMEM((STREAMS, LANES), F32),
            pltpu.VMEM((STREAMS, LANES), F32),
            pltpu.VMEM((STREAMS, PAIR_WIDTH), F32),
            pltpu.VMEM((nh, LANES), F32),
            pltpu.VMEM((PAGE_SIZE, 2 * PAIR_ROWS), BF16),
        ],
    )
    return pl.pallas_call(
        _fox_decode_kernel,
        grid_spec=grid_spec,
        out_shape=jax.ShapeDtypeStruct((nb, nh, hd), F32),
        compiler_params=_cparams(("arbitrary", "arbitrary")),
        name="fox_decode",
    )(page_table, q3, kn3, vn3, lfn3, cache_k, cache_v, cache_lf_t)


def _merge_kernel(x_ref, oa_ref, za_ref, ob_ref, zb_ref, ga_ref, gb_ref,
                  wua_ref, wub_ref, wo_ref, fnw_ref, y_ref):
    a = (oa_ref[...] * jax.nn.silu(za_ref[...].astype(F32))).astype(BF16)
    b = (ob_ref[...] * jax.nn.silu(zb_ref[...].astype(F32))).astype(BF16)
    ua = jnp.dot(a, wua_ref[...], preferred_element_type=F32)
    ub = jnp.dot(b, wub_ref[...], preferred_element_type=F32)
    merged = (jax.nn.sigmoid(ga_ref[...].astype(F32)) * ua
              + jax.nn.sigmoid(gb_ref[...].astype(F32)) * ub)
    out = x_ref[...] + jnp.dot(merged.astype(BF16), wo_ref[...], preferred_element_type=F32)
    ms = jnp.mean(out * out, axis=-1, keepdims=True)
    y_ref[...] = out * lax.rsqrt(ms + EPS) * fnw_ref[...]


def _merge(x, o_a, o_b, rest, w_up_a, w_up_b, w_o, final_norm_w, tm):
    m = x.shape[0]
    const = dict(pipeline_mode=pl.Buffered(1))
    return pl.pallas_call(
        _merge_kernel,
        grid=(m // tm,),
        in_specs=[
            pl.BlockSpec((tm, D_MODEL), lambda i: (i, 0)),
            pl.BlockSpec((tm, FOX_WIDTH), lambda i: (i, 0)),
            pl.BlockSpec((tm, FOX_WIDTH), lambda i: (i, REST_ZA // FOX_WIDTH)),
            pl.BlockSpec((tm, RET_WIDTH), lambda i: (i, 0)),
            pl.BlockSpec((tm, RET_WIDTH), lambda i: (i, REST_ZB // RET_WIDTH)),
            pl.BlockSpec((tm, D_MODEL), lambda i: (i, REST_GA // D_MODEL)),
            pl.BlockSpec((tm, D_MODEL), lambda i: (i, REST_GB // D_MODEL)),
            pl.BlockSpec((FOX_WIDTH, D_MODEL), lambda i: (0, 0), **const),
            pl.BlockSpec((RET_WIDTH, D_MODEL), lambda i: (0, 0), **const),
            pl.BlockSpec((D_MODEL, D_MODEL), lambda i: (0, 0), **const),
            pl.BlockSpec((1, D_MODEL), lambda i: (0, 0)),
        ],
        out_specs=pl.BlockSpec((tm, D_MODEL), lambda i: (i, 0)),
        out_shape=jax.ShapeDtypeStruct((m, D_MODEL), F32),
        compiler_params=_cparams(("arbitrary",)),
        name="merge",
    )(x, o_a, rest, o_b, rest, rest, rest, w_up_a, w_up_b, w_o, final_norm_w)


def _rope_tables(pos):
    half = RET_HEAD_DIM // 2
    inv = 1.0 / (ROPE_BASE ** (jnp.arange(half, dtype=F32) / half))
    ang = pos.astype(F32)[:, None] * inv[None, :]
    return jnp.cos(ang), jnp.sin(ang)


def _layer(xp, xs, cache_k, cache_v, cache_logf, state_ret, page_table,
           norm_w, w_in, b_f, ret_norm_w, w_up_a, w_up_b, w_o, final_norm_w):
    batch, seq, _ = xp.shape
    nb = xs.shape[0]
    n_pool = cache_k.shape[0]
    past = page_table.shape[1] * PAGE_SIZE
    nh, hd, fw, rw = FOX_HEADS, FOX_HEAD_DIM, FOX_WIDTH, RET_WIDTH

    wt = w_in.T
    bf_pad = jnp.pad(b_f, (0, F_ROWS - nh)).reshape(1, F_ROWS)
    wua, wub, wo = w_up_a.astype(BF16), w_up_b.astype(BF16), w_o.astype(BF16)
    nw = norm_w.reshape(1, D_MODEL)
    rnw = ret_norm_w.reshape(1, rw)
    fnw = final_norm_w.reshape(1, D_MODEL)
    log_gamma = jnp.log(1.0 - 2.0 ** (-5.0 - jnp.arange(RET_HEADS, dtype=F32)))
    cos_p, sin_p = _rope_tables(jnp.arange(seq, dtype=jnp.int32))
    cos_s, sin_s = _rope_tables(past + jnp.arange(1, dtype=jnp.int32))

    x2 = xp.reshape(batch * seq, D_MODEL)
    k_p, v_p, rest_p, f_p = _inproj(x2, nw, wt, tm=1024, rest_dtype=BF16)
    logf_p, c_p = _logf_cumsum(f_p, bf_pad, batch, seq)
    o_a = _fox_prompt(rest_p, k_p, v_p, c_p, batch, seq)
    o_b, st_p = _ret_prompt(log_gamma, rest_p, cos_p, sin_p, rnw, batch, seq)
    y_p = _merge(x2, o_a, o_b, rest_p, wua, wub, wo, fnw, tm=256)

    xs2 = xs.reshape(nb, D_MODEL)
    k_s, v_s, rest_s, f_s = _inproj(xs2, nw, wt, tm=nb, rest_dtype=F32)
    logf_s = _logf(f_s, bf_pad)
    rest_s3 = rest_s.reshape(nb, 1, REST_WIDTH)
    q_s = rest_s[:, REST_QA:REST_QA + fw].reshape(nb, nh, hd)
    lfn = jnp.broadcast_to(logf_s[:, :, None], (nb, nh, hd))
    o_a_s = _fox_decode(
        page_table, q_s, k_s.reshape(nb, nh, hd), v_s.reshape(nb, nh, hd), lfn,
        cache_k.reshape(n_pool, PAIRS, 2, nh, hd), cache_v.reshape(n_pool, PAIRS, 2, nh, hd),
        jnp.swapaxes(cache_logf, 1, 2))
    o_b_s, st_s = _ret_sample(log_gamma, rest_s3, cos_s, sin_s, rnw, state_ret)
    y_s = _merge(xs2, o_a_s.reshape(nb, fw), o_b_s.reshape(nb, rw), rest_s,
                 wua, wub, wo, fnw, tm=nb)

    hshape = (nh, hd)
    return (y_p.reshape(batch, seq, D_MODEL), y_s.reshape(nb, 1, D_MODEL),
            k_p.reshape(batch, seq, *hshape), v_p.reshape(batch, seq, *hshape),
            logf_p.reshape(batch, seq, nh), st_p,
            k_s.reshape(nb, 1, *hshape), v_s.reshape(nb, 1, *hshape),
            logf_s.reshape(nb, 1, nh), st_s)


def kernel(x_prompt, x_sample, cache_k, cache_v, cache_logf, state_ret, page_table,
           norm_w, w_in, b_f, ret_norm_w, w_up_a, w_up_b, w_o, final_norm_w):
    depth = w_in.shape[0]
    assert depth == 1 and x_sample.shape[1] == 1
    outs = _layer(x_prompt, x_sample, cache_k[0], cache_v[0], cache_logf[0], state_ret[0],
                  page_table, norm_w[0], w_in[0], b_f[0], ret_norm_w[0], w_up_a[0],
                  w_up_b[0], w_o[0], final_norm_w)
    y_p, y_s, k_p, v_p, lf_p, st_p, k_s, v_s, lf_s, st_s = outs
    stack = lambda a: a[None]
    return (y_p, y_s, stack(k_p), stack(v_p), stack(lf_p), stack(st_p),
            stack(k_s), stack(v_s), stack(lf_s), stack(st_s))
```

```python
import functools
import itertools
import math

import jax
import jax.numpy as jnp
from jax import lax
from jax.experimental import pallas as pl
from jax.experimental.pallas import tpu as pltpu

F32 = jnp.float32
BF16 = jnp.bfloat16

D_MODEL = 2048
FOX_HEADS = 8
FOX_HEAD_DIM = 128
FOX_WIDTH = FOX_HEADS * FOX_HEAD_DIM
RET_HEADS = 4
RET_HEAD_DIM = 256
RET_WIDTH = RET_HEADS * RET_HEAD_DIM
RET_CHUNK = 128
PAGE_SIZE = 128
ROPE_BASE = 10000.0
EPS = 1e-6
NEG_INF = -1e30
FOX_SCALE = FOX_HEAD_DIM ** -0.5
RET_K_SCALE = RET_HEAD_DIM ** -0.5
LOG2E = math.log2(math.e)

LANES = 128
SUBLANES = 8
VMEM_LIMIT = 56 * 1024 * 1024

REST_QA, REST_ZA, REST_QB, REST_KB, REST_VB, REST_ZB, REST_GA, REST_GB = (
    0, 1024, 2048, 3072, 4096, 5120, 6144, 8192)
REST_WIDTH = 10240
PROJ_TN = 512
N_KV_TILES = FOX_WIDTH // PROJ_TN
F_ROWS = LANES
W_STREAMS = 1

NT_DIMS = (((1,), (1,)), ((), ()))
TN_DIMS = (((0,), (0,)), ((), ()))


def _cparams(sem, vmem=VMEM_LIMIT):
    return pltpu.CompilerParams(dimension_semantics=sem, vmem_limit_bytes=vmem)


def _inproj_kernel(x_ref, nw_ref, *refs, weights_as_lhs):
    wt_refs = refs[:W_STREAMS]
    wf_ref, k_ref, v_ref, rest_ref, f_ref, h_ref = refs[W_STREAMS:]
    j = pl.program_id(1)
    m = x_ref.shape[0]

    def project(w_refs):
        w = jnp.concatenate([r[...] for r in w_refs], axis=0).astype(BF16)
        if weights_as_lhs:
            out_t = lax.dot_general(w, h_ref[...], NT_DIMS, preferred_element_type=F32)
            return out_t.T[:m]
        return lax.dot_general(h_ref[...], w, NT_DIMS, preferred_element_type=F32)

    @pl.when(j == 0)
    def _():
        x = x_ref[...]
        ms = jnp.mean(x * x, axis=-1, keepdims=True)
        h = (x * lax.rsqrt(ms + EPS) * nw_ref[...]).astype(BF16)
        if h_ref.shape[0] != m:
            h_ref[...] = jnp.zeros(h_ref.shape, BF16)
        h_ref[:m, :] = h
        f_ref[...] = project([wf_ref])

    acc = project(wt_refs)

    @pl.when(j < N_KV_TILES)
    def _():
        k_ref[...] = acc

    @pl.when((j >= N_KV_TILES) & (j < 2 * N_KV_TILES))
    def _():
        v_ref[...] = acc

    @pl.when(j >= 2 * N_KV_TILES)
    def _():
        rest_ref[...] = acc.astype(rest_ref.dtype)


def _wt_row_offset(j):
    t, n, fw = PROJ_TN, N_KV_TILES, FOX_WIDTH
    after_f = 4 * fw + FOX_HEADS
    return jnp.where(
        j < n, fw + j * t,
        jnp.where(j < 2 * n, 2 * fw + (j - n) * t,
                  jnp.where(j < 3 * n, (j - 2 * n) * t,
                            jnp.where(j < 4 * n, 3 * fw + (j - 3 * n) * t,
                                      after_f + (j - 4 * n) * t))))


def _inproj(x, norm_w, wt, tm, rest_dtype):
    m = x.shape[0]
    n_tiles = (wt.shape[0] - FOX_HEADS) // PROJ_TN
    kv_last = N_KV_TILES - 1
    weights_as_lhs = tm < LANES
    rows = PROJ_TN // W_STREAMS

    def wt_map(s, i, j):
        return (pl.multiple_of(_wt_row_offset(j) + s * rows, SUBLANES), 0)

    return pl.pallas_call(
        functools.partial(_inproj_kernel, weights_as_lhs=weights_as_lhs),
        grid=(m // tm, n_tiles),
        in_specs=[
            pl.BlockSpec((tm, D_MODEL), lambda i, j: (i, 0)),
            pl.BlockSpec((1, D_MODEL), lambda i, j: (0, 0)),
            *[pl.BlockSpec((pl.Element(rows), pl.Element(D_MODEL)), functools.partial(wt_map, s))
              for s in range(W_STREAMS)],
            pl.BlockSpec((F_ROWS, D_MODEL), lambda i, j: (4 * FOX_WIDTH // F_ROWS, 0)),
        ],
        out_specs=[
            pl.BlockSpec((tm, PROJ_TN), lambda i, j: (i, jnp.minimum(j, kv_last))),
            pl.BlockSpec((tm, PROJ_TN),
                         lambda i, j: (i, jnp.clip(j - N_KV_TILES, 0, kv_last))),
            pl.BlockSpec((tm, PROJ_TN),
                         lambda i, j: (i, jnp.maximum(j - 2 * N_KV_TILES, 0))),
            pl.BlockSpec((tm, F_ROWS), lambda i, j: (i, 0)),
        ],
        out_shape=[
            jax.ShapeDtypeStruct((m, FOX_WIDTH), F32),
            jax.ShapeDtypeStruct((m, FOX_WIDTH), F32),
            jax.ShapeDtypeStruct((m, REST_WIDTH), rest_dtype),
            jax.ShapeDtypeStruct((m, F_ROWS), F32),
        ],
        scratch_shapes=[pltpu.VMEM((max(tm, LANES), D_MODEL), BF16)],
        compiler_params=_cparams(("arbitrary", "arbitrary")),
        name="inproj",
    )(x, norm_w, *([wt] * W_STREAMS), wt)


def _logf_kernel(f_ref, bf_ref, logf_ref):
    lf = jax.nn.log_sigmoid(f_ref[...] + bf_ref[...])
    logf_ref[...] = lf[:, :FOX_HEADS]


def _logf(f, bf_pad):
    m = f.shape[0]
    return pl.pallas_call(
        _logf_kernel,
        grid=(1,),
        in_specs=[pl.BlockSpec((m, F_ROWS), lambda i: (0, 0)),
                  pl.BlockSpec((1, F_ROWS), lambda i: (0, 0))],
        out_specs=pl.BlockSpec((m, FOX_HEADS), lambda i: (0, 0)),
        out_shape=jax.ShapeDtypeStruct((m, FOX_HEADS), F32),
        name="logf_sample",
    )(f, bf_pad)


def _logf_cumsum_kernel(f_ref, bf_ref, logf_ref, c_ref):
    seq = f_ref.shape[0]
    lf = jax.nn.log_sigmoid(f_ref[...] + bf_ref[...])
    logf_ref[...] = lf[:, :FOX_HEADS]
    row = lax.broadcasted_iota(jnp.int32, (LANES, LANES), 0)
    col = lax.broadcasted_iota(jnp.int32, (LANES, LANES), 1)
    lower = (col <= row).astype(F32)
    carry = jnp.zeros((1, LANES), F32)
    for blk in range(seq // LANES):
        rows = slice(blk * LANES, (blk + 1) * LANES)
        cs = jnp.dot(lower, lf[rows, :], preferred_element_type=F32,
                     precision=lax.Precision.HIGHEST) + carry
        c_ref[rows, :] = cs
        carry = cs[LANES - 1:LANES, :]


def _logf_cumsum(f, bf_pad, batch, seq):
    return pl.pallas_call(
        _logf_cumsum_kernel,
        grid=(batch,),
        in_specs=[pl.BlockSpec((seq, F_ROWS), lambda b: (b, 0)),
                  pl.BlockSpec((1, F_ROWS), lambda b: (0, 0))],
        out_specs=[pl.BlockSpec((seq, FOX_HEADS), lambda b: (b, 0)),
                   pl.BlockSpec((seq, LANES), lambda b: (b, 0))],
        out_shape=[jax.ShapeDtypeStruct((batch * seq, FOX_HEADS), F32),
                   jax.ShapeDtypeStruct((batch * seq, LANES), F32)],
        compiler_params=_cparams(("arbitrary",)),
        name="logf_cumsum",
    )(f, bf_pad)


FOX_TQ = 512
FOX_TK = 512
FOX_HPS = 4
N_BIAS_FEATURES = 3


def _split3(x):
    hi = x.astype(BF16)
    r1 = x - hi.astype(F32)
    mid = r1.astype(BF16)
    lo = (r1 - mid.astype(F32)).astype(BF16)
    return hi, mid, lo


def _fox_prompt_kernel(q_ref, k_ref, v_ref, c_ref, o_ref, kx_ref, vt_ref):
    hg = pl.program_id(1)
    qi = pl.program_id(2)
    hd = FOX_HEAD_DIM
    seq = k_ref.shape[0]
    heads = range(FOX_HPS)

    @pl.when(qi == 0)
    def _():
        lane = lax.broadcasted_iota(jnp.int32, (seq, LANES), 1)
        for hh in heads:
            cols = slice(hh * hd, (hh + 1) * hd)
            c_h = jnp.sum(jnp.where(lane == hg * FOX_HPS + hh, c_ref[...], 0.0),
                          axis=1, keepdims=True)
            hi, mid, lo = (piece.astype(F32) for piece in _split3(
                jnp.broadcast_to(c_h * (1.0 / FOX_SCALE), (seq, LANES))))
            feat = jnp.where(lane == 0, hi, jnp.where(lane == 1, mid,
                                                      jnp.where(lane == 2, lo, 0.0)))
            kx_ref[hh, :, :hd] = k_ref[:, cols].astype(BF16)
            kx_ref[hh, :, hd:] = feat.astype(BF16)
            vt_ref[hh, :hd, :] = v_ref[:, cols].T.astype(BF16)
            vt_ref[hh, hd:, :] = jnp.ones((hd, seq), BF16)

    qlane = lax.broadcasted_iota(jnp.int32, (FOX_TQ, LANES), 1)
    qfeat = jnp.where(qlane < N_BIAS_FEATURES, -1.0, 0.0).astype(BF16)
    qx = [jnp.concatenate([q_ref[:, hh * hd:(hh + 1) * hd].astype(BF16), qfeat], axis=1)
          for hh in heads]

    def block(kbi, carry, diag_offset=None):
        start = pl.multiple_of(kbi * FOX_TK, FOX_TK)
        out = []
        for hh in heads:
            m, acc = carry[hh]
            kx = kx_ref[hh, pl.ds(start, FOX_TK), :]
            s = lax.dot_general(kx, qx[hh], NT_DIMS, preferred_element_type=F32)
            s = s * (FOX_SCALE * LOG2E)
            if diag_offset is not None:
                kpos = lax.broadcasted_iota(jnp.int32, (FOX_TK, FOX_TQ), 0) + diag_offset
                qpos = lax.broadcasted_iota(jnp.int32, (FOX_TK, FOX_TQ), 1)
                s = jnp.where(kpos <= qpos, s, NEG_INF)
            m_new = jnp.maximum(m, jnp.max(s, axis=0, keepdims=True))
            alpha = jnp.exp2(m - m_new)
            p = jnp.exp2(s - m_new)
            acc = alpha * acc + jnp.dot(vt_ref[hh, :, pl.ds(start, FOX_TK)], p.astype(BF16),
                                        preferred_element_type=F32)
            out.append((m_new, acc))
        return tuple(out)

    init = tuple((jnp.full((1, FOX_TQ), NEG_INF, F32), jnp.zeros((2 * hd, FOX_TQ), F32))
                 for _ in heads)
    per_q = FOX_TQ // FOX_TK
    final = lax.fori_loop(0, qi * per_q, block, init)
    for t in range(per_q):
        final = block(qi * per_q + t, final, diag_offset=t * FOX_TK)
    for hh in heads:
        acc = final[hh][1]
        o_ref[:, hh * hd:(hh + 1) * hd] = (acc[:hd] / acc[hd:]).T


def _fox_prompt(rest, k, v, c, batch, seq):
    nq = seq // FOX_TQ
    hw = FOX_HPS * FOX_HEAD_DIM
    return pl.pallas_call(
        _fox_prompt_kernel,
        grid=(batch, FOX_HEADS // FOX_HPS, nq),
        in_specs=[
            pl.BlockSpec((FOX_TQ, hw), lambda b, g, qi: (b * nq + qi, REST_QA // hw + g)),
            pl.BlockSpec((seq, hw), lambda b, g, qi: (b, g)),
            pl.BlockSpec((seq, hw), lambda b, g, qi: (b, g)),
            pl.BlockSpec((seq, LANES), lambda b, g, qi: (b, 0)),
        ],
        out_specs=pl.BlockSpec((FOX_TQ, hw), lambda b, g, qi: (b * nq + qi, g)),
        out_shape=jax.ShapeDtypeStruct((batch * seq, FOX_WIDTH), F32),
        scratch_shapes=[pltpu.VMEM((FOX_HPS, seq, 2 * FOX_HEAD_DIM), BF16),
                        pltpu.VMEM((FOX_HPS, 2 * FOX_HEAD_DIM, seq), BF16)],
        compiler_params=_cparams(("arbitrary", "arbitrary", "arbitrary")),
        name="fox_prompt",
    )(rest, k, v, c)


def _rotary(x, cos, sin):
    half = RET_HEAD_DIM // 2
    x1 = x[:, :half]
    x2 = x[:, half:]
    return jnp.concatenate([x1 * cos - x2 * sin, x2 * cos + x1 * sin], axis=-1)


def _head_rmsnorm(o, w):
    return o * lax.rsqrt(jnp.mean(o * o, axis=-1, keepdims=True) + EPS) * w


def _ret_prompt_kernel(lg_ref, q_ref, k_ref, v_ref, cos_ref, sin_ref, w_ref,
                       o_ref, st_ref):
    h = pl.program_id(1)
    lg = lg_ref[h]
    n_chunks = q_ref.shape[0] // RET_CHUNK
    ri = lax.broadcasted_iota(jnp.int32, (RET_CHUNK, RET_CHUNK), 0)
    ci = lax.broadcasted_iota(jnp.int32, (RET_CHUNK, RET_CHUNK), 1)
    diff = (ri - ci).astype(F32)
    decay = jnp.where(diff >= 0, jnp.exp(lg * jnp.maximum(diff, 0.0)), 0.0)
    pos = lax.broadcasted_iota(jnp.int32, (RET_CHUNK, RET_HEAD_DIM), 0).astype(F32)
    cross = jnp.exp(lg * (pos + 1.0))
    kdec = jnp.exp(lg * (RET_CHUNK - 1.0 - pos))
    g_chunk = jnp.exp(jnp.full((1, RET_HEAD_DIM), lg * RET_CHUNK, F32))
    w = w_ref[...]

    st_ref[0, 0] = jnp.zeros((RET_HEAD_DIM, RET_HEAD_DIM), F32)

    def chunk(c, _):
        r0 = pl.multiple_of(c * RET_CHUNK, RET_CHUNK)
        rows = pl.ds(r0, RET_CHUNK)
        cos = cos_ref[rows, :]
        sin = sin_ref[rows, :]
        qr = _rotary(q_ref[rows, :].astype(F32), cos, sin)
        kr = _rotary(k_ref[rows, :].astype(F32), cos, sin) * RET_K_SCALE
        qb = qr.astype(BF16)
        kb = kr.astype(BF16)
        vb = v_ref[rows, :].astype(BF16)
        st = st_ref[0, 0]
        scores = lax.dot_general(qb, kb, NT_DIMS, preferred_element_type=F32) * decay
        o = jnp.dot(scores.astype(BF16), vb, preferred_element_type=F32)
        o = o + cross * jnp.dot(qb, st.astype(BF16), preferred_element_type=F32)
        kd = (kr * kdec).astype(BF16)
        st_ref[0, 0] = g_chunk * st + lax.dot_general(kd, vb, TN_DIMS,
                                                      preferred_element_type=F32)
        o_ref[rows, :] = _head_rmsnorm(o, w)
        return 0

    lax.fori_loop(0, n_chunks, chunk, 0, unroll=4)


def _ret_prompt(log_gamma, rest, cos, sin, ret_norm_w, batch, seq):
    hd = RET_HEAD_DIM
    return pl.pallas_call(
        _ret_prompt_kernel,
        grid=(batch, RET_HEADS),
        in_specs=[
            pl.BlockSpec(memory_space=pltpu.SMEM),
            pl.BlockSpec((seq, hd), lambda b, h: (b, REST_QB // hd + h)),
            pl.BlockSpec((seq, hd), lambda b, h: (b, REST_KB // hd + h)),
            pl.BlockSpec((seq, hd), lambda b, h: (b, REST_VB // hd + h)),
            pl.BlockSpec((seq, hd // 2), lambda b, h: (0, 0)),
            pl.BlockSpec((seq, hd // 2), lambda b, h: (0, 0)),
            pl.BlockSpec((1, hd), lambda b, h: (0, h)),
        ],
        out_specs=[
            pl.BlockSpec((seq, hd), lambda b, h: (b, h)),
            pl.BlockSpec((1, 1, hd, hd), lambda b, h: (b, h, 0, 0)),
        ],
        out_shape=[
            jax.ShapeDtypeStruct((batch * seq, RET_WIDTH), F32),
            jax.ShapeDtypeStruct((batch, RET_HEADS, hd, hd), F32),
        ],
        compiler_params=_cparams(("arbitrary", "arbitrary")),
        name="ret_prompt",
    )(log_gamma, rest, rest, rest, cos, sin, ret_norm_w)


RET_SAMPLE_SEQS = 2


def _ret_sample_kernel(lg_ref, q_ref, k_ref, v_ref, cos_ref, sin_ref, w_ref, st_ref,
                       o_ref, nst_ref):
    hd = RET_HEAD_DIM
    cos = cos_ref[...]
    sin = sin_ref[...]
    first_row = lax.broadcasted_iota(jnp.int32, (16, hd), 0) == 0
    for s, h in itertools.product(range(q_ref.shape[0]), range(RET_HEADS)):
        cols = slice(h * hd, (h + 1) * hd)
        gamma = jnp.exp(jnp.full((1, hd), lg_ref[h], F32))
        qr = _rotary(q_ref[s, :, cols], cos, sin)
        kr = _rotary(k_ref[s, :, cols], cos, sin) * RET_K_SCALE
        vh = v_ref[s, :, cols]
        st = st_ref[s, h]
        q16 = jnp.broadcast_to(qr, (16, hd)).astype(BF16)
        qs = jnp.dot(q16, st.astype(BF16), preferred_element_type=F32)[0:1]
        score = jnp.sum(qr * kr, axis=-1, keepdims=True)
        o = score * vh + gamma * qs
        k16 = jnp.where(first_row, jnp.broadcast_to(kr, (16, hd)), 0.0).astype(BF16)
        v16 = jnp.broadcast_to(vh, (16, hd)).astype(BF16)
        nst_ref[s, h] = gamma * st + lax.dot_general(k16, v16, TN_DIMS,
                                                     preferred_element_type=F32)
        o_ref[s, :, cols] = _head_rmsnorm(o, w_ref[:, cols])


def _ret_sample(log_gamma, rest3, cos, sin, ret_norm_w, state):
    nb = rest3.shape[0]
    hd = RET_HEAD_DIM
    ns = RET_SAMPLE_SEQS
    return pl.pallas_call(
        _ret_sample_kernel,
        grid=(nb // ns,),
        in_specs=[
            pl.BlockSpec(memory_space=pltpu.SMEM),
            pl.BlockSpec((ns, 1, RET_WIDTH), lambda b: (b, 0, REST_QB // RET_WIDTH)),
            pl.BlockSpec((ns, 1, RET_WIDTH), lambda b: (b, 0, REST_KB // RET_WIDTH)),
            pl.BlockSpec((ns, 1, RET_WIDTH), lambda b: (b, 0, REST_VB // RET_WIDTH)),
            pl.BlockSpec((1, hd // 2), lambda b: (0, 0)),
            pl.BlockSpec((1, hd // 2), lambda b: (0, 0)),
            pl.BlockSpec((1, RET_WIDTH), lambda b: (0, 0)),
            pl.BlockSpec((ns, RET_HEADS, hd, hd), lambda b: (b, 0, 0, 0)),
        ],
        out_specs=[
            pl.BlockSpec((ns, 1, RET_WIDTH), lambda b: (b, 0, 0)),
            pl.BlockSpec((ns, RET_HEADS, hd, hd), lambda b: (b, 0, 0, 0)),
        ],
        out_shape=[
            jax.ShapeDtypeStruct((nb, 1, RET_WIDTH), F32),
            jax.ShapeDtypeStruct((nb, RET_HEADS, hd, hd), F32),
        ],
        compiler_params=_cparams(("arbitrary",)),
        name="ret_sample",
    )(log_gamma, rest3, rest3, rest3, cos, sin, ret_norm_w, state)


DEC_PAGES = 8
DEC_SLOTS = 4
DEC_GROUP = 2
PAIRS = PAGE_SIZE // 2
PAIR_ROWS = PAIRS * FOX_HEADS
PAIR_WIDTH = 2 * FOX_HEAD_DIM
STREAMS = 2 * FOX_HEADS
SEM_K, SEM_V, SEM_LF = 0, 1, 2


def _pair_view(page_ref):
    even = page_ref[:, 0].reshape(PAIR_ROWS, FOX_HEAD_DIM)
    odd = page_ref[:, 1].reshape(PAIR_ROWS, FOX_HEAD_DIM)
    return jnp.concatenate([even, odd], axis=1).astype(BF16)


def _fox_decode_kernel(pt_ref, q_ref, kn_ref, vn_ref, lfn_ref, ck_hbm, cv_hbm, clf_hbm,
                       o_ref, kbuf, vbuf, lfbuf, sems, m_ref, l_ref, acc_ref, carry_ref,
                       after_ref):
    np_ = DEC_PAGES
    hd = FOX_HEAD_DIM
    nh = FOX_HEADS
    b = pl.program_id(0)
    i = pl.program_id(1)
    steps_per_seq = pl.num_programs(1)
    chunks_per_seq = DEC_SLOTS * steps_per_seq
    total_chunks = pl.num_programs(0) * chunks_per_seq
    first_chunk = (b * steps_per_seq + i) * DEC_SLOTS

    def chunk_copies(chunk, slot):
        seq_id = chunk // chunks_per_seq
        first_page = (chunks_per_seq - 1 - chunk % chunks_per_seq) * np_
        copies = []
        for j in range(np_):
            page = pt_ref[seq_id, first_page + j]
            copies.append(pltpu.make_async_copy(ck_hbm.at[page], kbuf.at[slot, j],
                                                sems.at[slot, SEM_K]))
            copies.append(pltpu.make_async_copy(cv_hbm.at[page], vbuf.at[slot, j],
                                                sems.at[slot, SEM_V]))
            copies.append(pltpu.make_async_copy(clf_hbm.at[page], lfbuf.at[slot, j],
                                                sems.at[slot, SEM_LF]))
        return copies

    def start_chunk(chunk, slot):
        for cp in chunk_copies(chunk, slot):
            cp.start()

    def wait_chunk(chunk, slot):
        for cp in chunk_copies(chunk, slot):
            cp.wait()

    @pl.when((b == 0) & (i == 0))
    def _():
        prow = lax.broadcasted_iota(jnp.int32, (PAGE_SIZE, 2 * PAIR_ROWS), 0)
        lane = lax.broadcasted_iota(jnp.int32, (PAGE_SIZE, 2 * PAIR_ROWS), 1)
        pos = 2 * ((lane % PAIR_ROWS) // nh) + lane // PAIR_ROWS
        after_ref[...] = jnp.where(prow > pos, 1.0, 0.0).astype(BF16)
        for slot in range(DEC_SLOTS):
            start_chunk(slot, slot)

    @pl.when(i == 0)
    def _():
        m_ref[...] = jnp.full((STREAMS, LANES), NEG_INF, F32)
        l_ref[...] = jnp.zeros((STREAMS, LANES), F32)
        acc_ref[...] = jnp.zeros((STREAMS, PAIR_WIDTH), F32)
        carry_ref[...] = jnp.zeros((nh, LANES), F32)

    q = q_ref[...]
    qb = q.astype(BF16)
    zq = jnp.zeros((nh, hd), BF16)
    qw = jnp.concatenate([jnp.concatenate([qb, zq], axis=1),
                          jnp.concatenate([zq, qb], axis=1)], axis=0)
    srow = lax.broadcasted_iota(jnp.int32, (STREAMS, PAIR_ROWS), 0)
    slane = lax.broadcasted_iota(jnp.int32, (STREAMS, PAIR_ROWS), 1)
    own_head = (slane % nh) == (srow % nh)
    cn = lfn_ref[:, 0:1]

    def consume(slot, state):
        m, l, acc, carry = state
        lfs = lfbuf[slot].reshape(np_ * nh, PAGE_SIZE)
        after = after_ref[...]
        suffix = sum(jnp.dot(piece, after, preferred_element_type=F32)
                     for piece in _split3(lfs))
        totals = jnp.sum(lfs, axis=1, keepdims=True)
        head_bias = [None] * np_
        for j in reversed(range(np_)):
            head_bias[j] = carry + cn
            carry = carry + totals[j * nh:(j + 1) * nh]

        def logits(j):
            rows = slice(j * nh, (j + 1) * nh)
            kw = _pair_view(kbuf.at[slot, j])
            s = lax.dot_general(qw, kw, NT_DIMS, preferred_element_type=F32) * FOX_SCALE
            bias = jnp.concatenate([suffix[rows, :PAIR_ROWS] + head_bias[j],
                                    suffix[rows, PAIR_ROWS:] + head_bias[j]], axis=0)
            return jnp.where(own_head, s + bias, NEG_INF)

        def update(m, l, acc, pages, s_pages):
            m_new = m
            for s in s_pages:
                m_new = jnp.maximum(m_new, jnp.max(s, axis=1, keepdims=True))
            alpha = jnp.exp(m - m_new)
            l = alpha * l
            acc = alpha * acc
            for j, s in zip(pages, s_pages):
                p = jnp.exp(s - m_new)
                l = l + jnp.sum(p, axis=1, keepdims=True)
                acc = acc + jnp.dot(p.astype(BF16), _pair_view(vbuf.at[slot, j]),
                                    preferred_element_type=F32)
            return m_new, l, acc

        groups = [list(range(g + DEC_GROUP - 1, g - 1, -1))
                  for g in range(np_ - DEC_GROUP, -1, -DEC_GROUP)]
        s_next = [logits(j) for j in groups[0]]
        for g, pages in enumerate(groups):
            s_cur = s_next
            if g + 1 < len(groups):
                s_next = [logits(j) for j in groups[g + 1]]
            m, l, acc = update(m, l, acc, pages, s_cur)
        return m, l, acc, carry

    state = (m_ref[:, 0:1], l_ref[:, 0:1], acc_ref[...], carry_ref[:, 0:1])
    for slot in range(DEC_SLOTS):
        wait_chunk(first_chunk + slot, slot)
        state = consume(slot, state)
        refill = first_chunk + slot + DEC_SLOTS

        @pl.when(refill < total_chunks)
        def _():
            start_chunk(refill, slot)

    m, l, acc, carry = state
    m_ref[...] = jnp.broadcast_to(m, (STREAMS, LANES))
    l_ref[...] = jnp.broadcast_to(l, (STREAMS, LANES))
    acc_ref[...] = acc
    carry_ref[...] = jnp.broadcast_to(carry, (nh, LANES))

    @pl.when(i == steps_per_seq - 1)
    def _():
        sn = jnp.sum(q * kn_ref[...], axis=1, keepdims=True) * FOX_SCALE
        m_e, m_o = m[:nh], m[nh:]
        m_f = jnp.maximum(jnp.maximum(m_e, m_o), sn)
        w_e = jnp.exp(m_e - m_f)
        w_o = jnp.exp(m_o - m_f)
        p_n = jnp.exp(sn - m_f)
        denom = w_e * l[:nh] + w_o * l[nh:] + p_n
        out = w_e * acc[:nh, :hd] + w_o * acc[nh:, hd:] + p_n * vn_ref[...]
        o_ref[...] = out / denom


def _fox_decode(page_table, q3, kn3, vn3, lfn3, cache_k, cache_v, cache_lf_t):
    nb, n_pages = page_table.shape
    steps_per_seq = n_pages // (DEC_PAGES * DEC_SLOTS)
    nh, hd = FOX_HEADS, FOX_HEAD_DIM
    head_spec = pl.BlockSpec((None, nh, hd), lambda b, i, pt: (b, 0, 0))
    hbm_spec = pl.BlockSpec(memory_space=pl.ANY)
    grid_spec = pltpu.PrefetchScalarGridSpec(
        num_scalar_prefetch=1,
        grid=(nb, steps_per_seq),
        in_specs=[head_spec] * 4 + [hbm_spec] * 3,
        out_specs=head_spec,
        scratch_shapes=[
            pltpu.VMEM((DEC_SLOTS, DEC_PAGES, PAIRS, 2, nh, hd), F32),
            pltpu.VMEM((DEC_SLOTS, DEC_PAGES, PAIRS, 2, nh, hd), F32),
            pltpu.VMEM((DEC_SLOTS, DEC_PAGES, nh, PAGE_SIZE), F32),
            pltpu.SemaphoreType.DMA((DEC_SLOTS, 3)),
            pltpu.VMEM((STREAMS, LANES), F32),
            pltpu.VMEM((STREAMS, LANES), F32),
            pltpu.VMEM((STREAMS, PAIR_WIDTH), F32),
            pltpu.VMEM((nh, LANES), F32),
            pltpu.VMEM((PAGE_SIZE, 2 * PAIR_ROWS), BF16),
        ],
    )
    return pl.pallas_call(
        _fox_decode_kernel,
        grid_spec=grid_spec,
        out_shape=jax.ShapeDtypeStruct((nb, nh, hd), F32),
        compiler_params=_cparams(("arbitrary", "arbitrary")),
        name="fox_decode",
    )(page_table, q3, kn3, vn3, lfn3, cache_k, cache_v, cache_lf_t)


def _merge_kernel(x_ref, oa_ref, za_ref, ob_ref, zb_ref, ga_ref, gb_ref,
                  wua_ref, wub_ref, wo_ref, fnw_ref, y_ref):
    a = (oa_ref[...] * jax.nn.silu(za_ref[...].astype(F32))).astype(BF16)
    b = (ob_ref[...] * jax.nn.silu(zb_ref[...].astype(F32))).astype(BF16)
    ua = jnp.dot(a, wua_ref[...], preferred_element_type=F32)
    ub = jnp.dot(b, wub_ref[...], preferred_element_type=F32)
    merged = (jax.nn.sigmoid(ga_ref[...].astype(F32)) * ua
              + jax.nn.sigmoid(gb_ref[...].astype(F32)) * ub)
    out = x_ref[...] + jnp.dot(merged.astype(BF16), wo_ref[...], preferred_element_type=F32)
    ms = jnp.mean(out * out, axis=-1, keepdims=True)
    y_ref[...] = out * lax.rsqrt(ms + EPS) * fnw_ref[...]


def _merge(x, o_a, o_b, rest, w_up_a, w_up_b, w_o, final_norm_w, tm):
    m = x.shape[0]
    const = dict(pipeline_mode=pl.Buffered(1))
    return pl.pallas_call(
        _merge_kernel,
        grid=(m // tm,),
        in_specs=[
            pl.BlockSpec((tm, D_MODEL), lambda i: (i, 0)),
            pl.BlockSpec((tm, FOX_WIDTH), lambda i: (i, 0)),
            pl.BlockSpec((tm, FOX_WIDTH), lambda i: (i, REST_ZA // FOX_WIDTH)),
            pl.BlockSpec((tm, RET_WIDTH), lambda i: (i, 0)),
            pl.BlockSpec((tm, RET_WIDTH), lambda i: (i, REST_ZB // RET_WIDTH)),
            pl.BlockSpec((tm, D_MODEL), lambda i: (i, REST_GA // D_MODEL)),
            pl.BlockSpec((tm, D_MODEL), lambda i: (i, REST_GB // D_MODEL)),
            pl.BlockSpec((FOX_WIDTH, D_MODEL), lambda i: (0, 0), **const),
            pl.BlockSpec((RET_WIDTH, D_MODEL), lambda i: (0, 0), **const),
            pl.BlockSpec((D_MODEL, D_MODEL), lambda i: (0, 0), **const),
            pl.BlockSpec((1, D_MODEL), lambda i: (0, 0)),
        ],
        out_specs=pl.BlockSpec((tm, D_MODEL), lambda i: (i, 0)),
        out_shape=jax.ShapeDtypeStruct((m, D_MODEL), F32),
        compiler_params=_cparams(("arbitrary",)),
        name="merge",
    )(x, o_a, rest, o_b, rest, rest, rest, w_up_a, w_up_b, w_o, final_norm_w)


def _rope_tables(pos):
    half = RET_HEAD_DIM // 2
    inv = 1.0 / (ROPE_BASE ** (jnp.arange(half, dtype=F32) / half))
    ang = pos.astype(F32)[:, None] * inv[None, :]
    return jnp.cos(ang), jnp.sin(ang)


def _layer(xp, xs, cache_k, cache_v, cache_logf, state_ret, page_table,
           norm_w, w_in, b_f, ret_norm_w, w_up_a, w_up_b, w_o, final_norm_w):
    batch, seq, _ = xp.shape
    nb = xs.shape[0]
    n_pool = cache_k.shape[0]
    past = page_table.shape[1] * PAGE_SIZE
    nh, hd, fw, rw = FOX_HEADS, FOX_HEAD_DIM, FOX_WIDTH, RET_WIDTH

    wt = w_in.T
    bf_pad = jnp.pad(b_f, (0, F_ROWS - nh)).reshape(1, F_ROWS)
    wua, wub, wo = w_up_a.astype(BF16), w_up_b.astype(BF16), w_o.astype(BF16)
    nw = norm_w.reshape(1, D_MODEL)
    rnw = ret_norm_w.reshape(1, rw)
    fnw = final_norm_w.reshape(1, D_MODEL)
    log_gamma = jnp.log(1.0 - 2.0 ** (-5.0 - jnp.arange(RET_HEADS, dtype=F32)))
    cos_p, sin_p = _rope_tables(jnp.arange(seq, dtype=jnp.int32))
    cos_s, sin_s = _rope_tables(past + jnp.arange(1, dtype=jnp.int32))

    x2 = xp.reshape(batch * seq, D_MODEL)
    k_p, v_p, rest_p, f_p = _inproj(x2, nw, wt, tm=1024, rest_dtype=BF16)
    logf_p, c_p = _logf_cumsum(f_p, bf_pad, batch, seq)
    o_a = _fox_prompt(rest_p, k_p, v_p, c_p, batch, seq)
    o_b, st_p = _ret_prompt(log_gamma, rest_p, cos_p, sin_p, rnw, batch, seq)
    y_p = _merge(x2, o_a, o_b, rest_p, wua, wub, wo, fnw, tm=256)

    xs2 = xs.reshape(nb, D_MODEL)
    k_s, v_s, rest_s, f_s = _inproj(xs2, nw, wt, tm=nb, rest_dtype=F32)
    logf_s = _logf(f_s, bf_pad)
    rest_s3 = rest_s.reshape(nb, 1, REST_WIDTH)
    q_s = rest_s[:, REST_QA:REST_QA + fw].reshape(nb, nh, hd)
    lfn = jnp.broadcast_to(logf_s[:, :, None], (nb, nh, hd))
    o_a_s = _fox_decode(
        page_table, q_s, k_s.reshape(nb, nh, hd), v_s.reshape(nb, nh, hd), lfn,
        cache_k.reshape(n_pool, PAIRS, 2, nh, hd), cache_v.reshape(n_pool, PAIRS, 2, nh, hd),
        jnp.swapaxes(cache_logf, 1, 2))
    o_b_s, st_s = _ret_sample(log_gamma, rest_s3, cos_s, sin_s, rnw, state_ret)
    y_s = _merge(xs2, o_a_s.reshape(nb, fw), o_b_s.reshape(nb, rw), rest_s,
                 wua, wub, wo, fnw, tm=nb)

    hshape = (nh, hd)
    return (y_p.reshape(batch, seq, D_MODEL), y_s.reshape(nb, 1, D_MODEL),
            k_p.reshape(batch, seq, *hshape), v_p.reshape(batch, seq, *hshape),
            logf_p.reshape(batch, seq, nh), st_p,
            k_s.reshape(nb, 1, *hshape), v_s.reshape(nb, 1, *hshape),
            logf_s.reshape(nb, 1, nh), st_s)


def kernel(x_prompt, x_sample, cache_k, cache_v, cache_logf, state_ret, page_table,
           norm_w, w_in, b_f, ret_norm_w, w_up_a, w_up_b, w_o, final_norm_w):
    depth = w_in.shape[0]
    assert depth == 1 and x_sample.shape[1] == 1
    outs = _layer(x_prompt, x_sample, cache_k[0], cache_v[0], cache_logf[0], state_ret[0],
                  page_table, norm_w[0], w_in[0], b_f[0], ret_norm_w[0], w_up_a[0],
                  w_up_b[0], w_o[0], final_norm_w)
    y_p, y_s, k_p, v_p, lf_p, st_p, k_s, v_s, lf_s, st_s = outs
    stack = lambda a: a[None]
    return (y_p, y_s, stack(k_p), stack(v_p), stack(lf_p), stack(st_p),
            stack(k_s), stack(v_s), stack(lf_s), stack(st_s))
```

```python
import functools
import itertools
import math

import jax
import jax.numpy as jnp
from jax import lax
from jax.experimental import pallas as pl
from jax.experimental.pallas import tpu as pltpu

F32 = jnp.float32
BF16 = jnp.bfloat16

D_MODEL = 2048
FOX_HEADS = 8
FOX_HEAD_DIM = 128
FOX_WIDTH = FOX_HEADS * FOX_HEAD_DIM
RET_HEADS = 4
RET_HEAD_DIM = 256
RET_WIDTH = RET_HEADS * RET_HEAD_DIM
RET_CHUNK = 128
PAGE_SIZE = 128
ROPE_BASE = 10000.0
EPS = 1e-6
NEG_INF = -1e30
FOX_SCALE = FOX_HEAD_DIM ** -0.5
RET_K_SCALE = RET_HEAD_DIM ** -0.5
LOG2E = math.log2(math.e)

LANES = 128
SUBLANES = 8
VMEM_LIMIT = 56 * 1024 * 1024

REST_QA, REST_ZA, REST_QB, REST_KB, REST_VB, REST_ZB, REST_GA, REST_GB = (
    0, 1024, 2048, 3072, 4096, 5120, 6144, 8192)
REST_WIDTH = 10240
PROJ_TN = 1024
N_KV_TILES = FOX_WIDTH // PROJ_TN
F_ROWS = LANES
W_STREAMS = 1

NT_DIMS = (((1,), (1,)), ((), ()))
TN_DIMS = (((0,), (0,)), ((), ()))


def _cparams(sem, vmem=VMEM_LIMIT):
    return pltpu.CompilerParams(dimension_semantics=sem, vmem_limit_bytes=vmem)


def _inproj_kernel(x_ref, nw_ref, *refs, weights_as_lhs):
    wt_refs = refs[:W_STREAMS]
    wf_ref, k_ref, v_ref, rest_ref, f_ref, h_ref = refs[W_STREAMS:]
    j = pl.program_id(1)
    m = x_ref.shape[0]

    def project(w_refs):
        w = jnp.concatenate([r[...] for r in w_refs], axis=0).astype(BF16)
        if weights_as_lhs:
            out_t = lax.dot_general(w, h_ref[...], NT_DIMS, preferred_element_type=F32)
            return out_t.T[:m]
        return lax.dot_general(h_ref[...], w, NT_DIMS, preferred_element_type=F32)

    @pl.when(j == 0)
    def _():
        x = x_ref[...]
        ms = jnp.mean(x * x, axis=-1, keepdims=True)
        h = (x * lax.rsqrt(ms + EPS) * nw_ref[...]).astype(BF16)
        if h_ref.shape[0] != m:
            h_ref[...] = jnp.zeros(h_ref.shape, BF16)
        h_ref[:m, :] = h
        f_ref[...] = project([wf_ref])

    acc = project(wt_refs)

    @pl.when(j < N_KV_TILES)
    def _():
        k_ref[...] = acc

    @pl.when((j >= N_KV_TILES) & (j < 2 * N_KV_TILES))
    def _():
        v_ref[...] = acc

    @pl.when(j >= 2 * N_KV_TILES)
    def _():
        rest_ref[...] = acc.astype(rest_ref.dtype)


def _wt_row_offset(j):
    t, n, fw = PROJ_TN, N_KV_TILES, FOX_WIDTH
    after_f = 4 * fw + FOX_HEADS
    return jnp.where(
        j < n, fw + j * t,
        jnp.where(j < 2 * n, 2 * fw + (j - n) * t,
                  jnp.where(j < 3 * n, (j - 2 * n) * t,
                            jnp.where(j < 4 * n, 3 * fw + (j - 3 * n) * t,
                                      after_f + (j - 4 * n) * t))))


def _inproj(x, norm_w, wt, tm, rest_dtype):
    m = x.shape[0]
    n_tiles = (wt.shape[0] - FOX_HEADS) // PROJ_TN
    kv_last = N_KV_TILES - 1
    weights_as_lhs = tm < LANES
    rows = PROJ_TN // W_STREAMS
    once = dict(pipeline_mode=pl.Buffered(1))

    def wt_map(s, i, j):
        return (pl.multiple_of(_wt_row_offset(j) + s * rows, SUBLANES), 0)

    return pl.pallas_call(
        functools.partial(_inproj_kernel, weights_as_lhs=weights_as_lhs),
        grid=(m // tm, n_tiles),
        in_specs=[
            pl.BlockSpec((tm, D_MODEL), lambda i, j: (i, 0), **once),
            pl.BlockSpec((1, D_MODEL), lambda i, j: (0, 0)),
            *[pl.BlockSpec((pl.Element(rows), pl.Element(D_MODEL)), functools.partial(wt_map, s))
              for s in range(W_STREAMS)],
            pl.BlockSpec((F_ROWS, D_MODEL), lambda i, j: (4 * FOX_WIDTH // F_ROWS, 0), **once),
        ],
        out_specs=[
            pl.BlockSpec((tm, PROJ_TN), lambda i, j: (i, jnp.minimum(j, kv_last)), **once),
            pl.BlockSpec((tm, PROJ_TN),
                         lambda i, j: (i, jnp.clip(j - N_KV_TILES, 0, kv_last)), **once),
            pl.BlockSpec((tm, PROJ_TN),
                         lambda i, j: (i, jnp.maximum(j - 2 * N_KV_TILES, 0))),
            pl.BlockSpec((tm, F_ROWS), lambda i, j: (i, 0)),
        ],
        out_shape=[
            jax.ShapeDtypeStruct((m, FOX_WIDTH), F32),
            jax.ShapeDtypeStruct((m, FOX_WIDTH), F32),
            jax.ShapeDtypeStruct((m, REST_WIDTH), rest_dtype),
            jax.ShapeDtypeStruct((m, F_ROWS), F32),
        ],
        scratch_shapes=[pltpu.VMEM((max(tm, LANES), D_MODEL), BF16)],
        compiler_params=_cparams(("arbitrary", "arbitrary")),
        name="inproj",
    )(x, norm_w, *([wt] * W_STREAMS), wt)


def _logf_kernel(f_ref, bf_ref, logf_ref):
    lf = jax.nn.log_sigmoid(f_ref[...] + bf_ref[...])
    logf_ref[...] = lf[:, :FOX_HEADS]


def _logf(f, bf_pad):
    m = f.shape[0]
    return pl.pallas_call(
        _logf_kernel,
        grid=(1,),
        in_specs=[pl.BlockSpec((m, F_ROWS), lambda i: (0, 0)),
                  pl.BlockSpec((1, F_ROWS), lambda i: (0, 0))],
        out_specs=pl.BlockSpec((m, FOX_HEADS), lambda i: (0, 0)),
        out_shape=jax.ShapeDtypeStruct((m, FOX_HEADS), F32),
        name="logf_sample",
    )(f, bf_pad)


def _logf_cumsum_kernel(f_ref, bf_ref, logf_ref, c_ref):
    seq = f_ref.shape[0]
    lf = jax.nn.log_sigmoid(f_ref[...] + bf_ref[...])
    logf_ref[...] = lf[:, :FOX_HEADS]
    row = lax.broadcasted_iota(jnp.int32, (LANES, LANES), 0)
    col = lax.broadcasted_iota(jnp.int32, (LANES, LANES), 1)
    lower = (col <= row).astype(F32)
    carry = jnp.zeros((1, LANES), F32)
    for blk in range(seq // LANES):
        rows = slice(blk * LANES, (blk + 1) * LANES)
        cs = jnp.dot(lower, lf[rows, :], preferred_element_type=F32,
                     precision=lax.Precision.HIGHEST) + carry
        c_ref[rows, :] = cs
        carry = cs[LANES - 1:LANES, :]


def _logf_cumsum(f, bf_pad, batch, seq):
    return pl.pallas_call(
        _logf_cumsum_kernel,
        grid=(batch,),
        in_specs=[pl.BlockSpec((seq, F_ROWS), lambda b: (b, 0)),
                  pl.BlockSpec((1, F_ROWS), lambda b: (0, 0))],
        out_specs=[pl.BlockSpec((seq, FOX_HEADS), lambda b: (b, 0)),
                   pl.BlockSpec((seq, LANES), lambda b: (b, 0))],
        out_shape=[jax.ShapeDtypeStruct((batch * seq, FOX_HEADS), F32),
                   jax.ShapeDtypeStruct((batch * seq, LANES), F32)],
        compiler_params=_cparams(("arbitrary",)),
        name="logf_cumsum",
    )(f, bf_pad)


FOX_TQ = 512
FOX_TK = 512
FOX_HPS = 4
N_BIAS_FEATURES = 3


def _split3(x):
    hi = x.astype(BF16)
    r1 = x - hi.astype(F32)
    mid = r1.astype(BF16)
    lo = (r1 - mid.astype(F32)).astype(BF16)
    return hi, mid, lo


def _fox_prompt_kernel(q_ref, k_ref, v_ref, c_ref, o_ref, kx_ref, vt_ref):
    hg = pl.program_id(1)
    qi = pl.program_id(2)
    hd = FOX_HEAD_DIM
    seq = k_ref.shape[0]
    heads = range(FOX_HPS)

    @pl.when(qi == 0)
    def _():
        lane = lax.broadcasted_iota(jnp.int32, (seq, LANES), 1)
        for hh in heads:
            cols = slice(hh * hd, (hh + 1) * hd)
            c_h = jnp.sum(jnp.where(lane == hg * FOX_HPS + hh, c_ref[...], 0.0),
                          axis=1, keepdims=True)
            hi, mid, lo = (piece.astype(F32) for piece in _split3(
                jnp.broadcast_to(c_h * (1.0 / FOX_SCALE), (seq, LANES))))
            feat = jnp.where(lane == 0, hi, jnp.where(lane == 1, mid,
                                                      jnp.where(lane == 2, lo, 0.0)))
            kx_ref[hh, :, :hd] = k_ref[:, cols].astype(BF16)
            kx_ref[hh, :, hd:] = feat.astype(BF16)
            vt_ref[hh, :hd, :] = v_ref[:, cols].T.astype(BF16)
            vt_ref[hh, hd:, :] = jnp.ones((hd, seq), BF16)

    qlane = lax.broadcasted_iota(jnp.int32, (FOX_TQ, LANES), 1)
    qfeat = jnp.where(qlane < N_BIAS_FEATURES, -1.0, 0.0).astype(BF16)
    qx = [jnp.concatenate([q_ref[:, hh * hd:(hh + 1) * hd].astype(BF16), qfeat], axis=1)
          for hh in heads]

    def block(kbi, carry, diag_offset=None):
        start = pl.multiple_of(kbi * FOX_TK, FOX_TK)
        out = []
        for hh in heads:
            m, acc = carry[hh]
            kx = kx_ref[hh, pl.ds(start, FOX_TK), :]
            s = lax.dot_general(kx, qx[hh], NT_DIMS, preferred_element_type=F32)
            s = s * (FOX_SCALE * LOG2E)
            if diag_offset is not None:
                kpos = lax.broadcasted_iota(jnp.int32, (FOX_TK, FOX_TQ), 0) + diag_offset
                qpos = lax.broadcasted_iota(jnp.int32, (FOX_TK, FOX_TQ), 1)
                s = jnp.where(kpos <= qpos, s, NEG_INF)
            m_new = jnp.maximum(m, jnp.max(s, axis=0, keepdims=True))
            alpha = jnp.exp2(m - m_new)
            p = jnp.exp2(s - m_new)
            acc = alpha * acc + jnp.dot(vt_ref[hh, :, pl.ds(start, FOX_TK)], p.astype(BF16),
                                        preferred_element_type=F32)
            out.append((m_new, acc))
        return tuple(out)

    init = tuple((jnp.full((1, FOX_TQ), NEG_INF, F32), jnp.zeros((2 * hd, FOX_TQ), F32))
                 for _ in heads)
    per_q = FOX_TQ // FOX_TK
    final = lax.fori_loop(0, qi * per_q, block, init)
    for t in range(per_q):
        final = block(qi * per_q + t, final, diag_offset=t * FOX_TK)
    for hh in heads:
        acc = final[hh][1]
        o_ref[:, hh * hd:(hh + 1) * hd] = (acc[:hd] / acc[hd:]).T


def _fox_prompt(rest, k, v, c, batch, seq):
    nq = seq // FOX_TQ
    hw = FOX_HPS * FOX_HEAD_DIM
    return pl.pallas_call(
        _fox_prompt_kernel,
        grid=(batch, FOX_HEADS // FOX_HPS, nq),
        in_specs=[
            pl.BlockSpec((FOX_TQ, hw), lambda b, g, qi: (b * nq + qi, REST_QA // hw + g)),
            pl.BlockSpec((seq, hw), lambda b, g, qi: (b, g)),
            pl.BlockSpec((seq, hw), lambda b, g, qi: (b, g)),
            pl.BlockSpec((seq, LANES), lambda b, g, qi: (b, 0)),
        ],
        out_specs=pl.BlockSpec((FOX_TQ, hw), lambda b, g, qi: (b * nq + qi, g)),
        out_shape=jax.ShapeDtypeStruct((batch * seq, FOX_WIDTH), F32),
        scratch_shapes=[pltpu.VMEM((FOX_HPS, seq, 2 * FOX_HEAD_DIM), BF16),
                        pltpu.VMEM((FOX_HPS, 2 * FOX_HEAD_DIM, seq), BF16)],
        compiler_params=_cparams(("arbitrary", "arbitrary", "arbitrary")),
        name="fox_prompt",
    )(rest, k, v, c)


def _rotary(x, cos, sin):
    half = RET_HEAD_DIM // 2
    x1 = x[:, :half]
    x2 = x[:, half:]
    return jnp.concatenate([x1 * cos - x2 * sin, x2 * cos + x1 * sin], axis=-1)


def _head_rmsnorm(o, w):
    return o * lax.rsqrt(jnp.mean(o * o, axis=-1, keepdims=True) + EPS) * w


def _ret_prompt_kernel(lg_ref, q_ref, k_ref, v_ref, cos_ref, sin_ref, w_ref,
                       o_ref, st_ref):
    h = pl.program_id(1)
    lg = lg_ref[h]
    n_chunks = q_ref.shape[0] // RET_CHUNK
    ri = lax.broadcasted_iota(jnp.int32, (RET_CHUNK, RET_CHUNK), 0)
    ci = lax.broadcasted_iota(jnp.int32, (RET_CHUNK, RET_CHUNK), 1)
    diff = (ri - ci).astype(F32)
    decay = jnp.where(diff >= 0, jnp.exp(lg * jnp.maximum(diff, 0.0)), 0.0)
    pos = lax.broadcasted_iota(jnp.int32, (RET_CHUNK, RET_HEAD_DIM), 0).astype(F32)
    cross = jnp.exp(lg * (pos + 1.0))
    kdec = jnp.exp(lg * (RET_CHUNK - 1.0 - pos))
    g_chunk = jnp.exp(jnp.full((1, RET_HEAD_DIM), lg * RET_CHUNK, F32))
    w = w_ref[...]

    st_ref[0, 0] = jnp.zeros((RET_HEAD_DIM, RET_HEAD_DIM), F32)

    def chunk(c, _):
        r0 = pl.multiple_of(c * RET_CHUNK, RET_CHUNK)
        rows = pl.ds(r0, RET_CHUNK)
        cos = cos_ref[rows, :]
        sin = sin_ref[rows, :]
        qr = _rotary(q_ref[rows, :].astype(F32), cos, sin)
        kr = _rotary(k_ref[rows, :].astype(F32), cos, sin) * RET_K_SCALE
        qb = qr.astype(BF16)
        kb = kr.astype(BF16)
        vb = v_ref[rows, :].astype(BF16)
        st = st_ref[0, 0]
        scores = lax.dot_general(qb, kb, NT_DIMS, preferred_element_type=F32) * decay
        o = jnp.dot(scores.astype(BF16), vb, preferred_element_type=F32)
        o = o + cross * jnp.dot(qb, st.astype(BF16), preferred_element_type=F32)
        kd = (kr * kdec).astype(BF16)
        st_ref[0, 0] = g_chunk * st + lax.dot_general(kd, vb, TN_DIMS,
                                                      preferred_element_type=F32)
        o_ref[rows, :] = _head_rmsnorm(o, w)
        return 0

    lax.fori_loop(0, n_chunks, chunk, 0, unroll=4)


def _ret_prompt(log_gamma, rest, cos, sin, ret_norm_w, batch, seq):
    hd = RET_HEAD_DIM
    return pl.pallas_call(
        _ret_prompt_kernel,
        grid=(batch, RET_HEADS),
        in_specs=[
            pl.BlockSpec(memory_space=pltpu.SMEM),
            pl.BlockSpec((seq, hd), lambda b, h: (b, REST_QB // hd + h)),
            pl.BlockSpec((seq, hd), lambda b, h: (b, REST_KB // hd + h)),
            pl.BlockSpec((seq, hd), lambda b, h: (b, REST_VB // hd + h)),
            pl.BlockSpec((seq, hd // 2), lambda b, h: (0, 0)),
            pl.BlockSpec((seq, hd // 2), lambda b, h: (0, 0)),
            pl.BlockSpec((1, hd), lambda b, h: (0, h)),
        ],
        out_specs=[
            pl.BlockSpec((seq, hd), lambda b, h: (b, h)),
            pl.BlockSpec((1, 1, hd, hd), lambda b, h: (b, h, 0, 0)),
        ],
        out_shape=[
            jax.ShapeDtypeStruct((batch * seq, RET_WIDTH), F32),
            jax.ShapeDtypeStruct((batch, RET_HEADS, hd, hd), F32),
        ],
        compiler_params=_cparams(("arbitrary", "arbitrary")),
        name="ret_prompt",
    )(log_gamma, rest, rest, rest, cos, sin, ret_norm_w)


RET_SAMPLE_SEQS = 2


def _ret_sample_kernel(lg_ref, q_ref, k_ref, v_ref, cos_ref, sin_ref, w_ref, st_ref,
                       o_ref, nst_ref):
    hd = RET_HEAD_DIM
    cos = cos_ref[...]
    sin = sin_ref[...]
    first_row = lax.broadcasted_iota(jnp.int32, (16, hd), 0) == 0
    for s, h in itertools.product(range(q_ref.shape[0]), range(RET_HEADS)):
        cols = slice(h * hd, (h + 1) * hd)
        gamma = jnp.exp(jnp.full((1, hd), lg_ref[h], F32))
        qr = _rotary(q_ref[s, :, cols], cos, sin)
        kr = _rotary(k_ref[s, :, cols], cos, sin) * RET_K_SCALE
        vh = v_ref[s, :, cols]
        st = st_ref[s, h]
        q16 = jnp.broadcast_to(qr, (16, hd)).astype(BF16)
        qs = jnp.dot(q16, st.astype(BF16), preferred_element_type=F32)[0:1]
        score = jnp.sum(qr * kr, axis=-1, keepdims=True)
        o = score * vh + gamma * qs
        k16 = jnp.where(first_row, jnp.broadcast_to(kr, (16, hd)), 0.0).astype(BF16)
        v16 = jnp.broadcast_to(vh, (16, hd)).astype(BF16)
        nst_ref[s, h] = gamma * st + lax.dot_general(k16, v16, TN_DIMS,
                                                     preferred_element_type=F32)
        o_ref[s, :, cols] = _head_rmsnorm(o, w_ref[:, cols])


def _ret_sample(log_gamma, rest3, cos, sin, ret_norm_w, state):
    nb = rest3.shape[0]
    hd = RET_HEAD_DIM
    ns = RET_SAMPLE_SEQS
    return pl.pallas_call(
        _ret_sample_kernel,
        grid=(nb // ns,),
        in_specs=[
            pl.BlockSpec(memory_space=pltpu.SMEM),
            pl.BlockSpec((ns, 1, RET_WIDTH), lambda b: (b, 0, REST_QB // RET_WIDTH)),
            pl.BlockSpec((ns, 1, RET_WIDTH), lambda b: (b, 0, REST_KB // RET_WIDTH)),
            pl.BlockSpec((ns, 1, RET_WIDTH), lambda b: (b, 0, REST_VB // RET_WIDTH)),
            pl.BlockSpec((1, hd // 2), lambda b: (0, 0)),
            pl.BlockSpec((1, hd // 2), lambda b: (0, 0)),
            pl.BlockSpec((1, RET_WIDTH), lambda b: (0, 0)),
            pl.BlockSpec((ns, RET_HEADS, hd, hd), lambda b: (b, 0, 0, 0)),
        ],
        out_specs=[
            pl.BlockSpec((ns, 1, RET_WIDTH), lambda b: (b, 0, 0)),
            pl.BlockSpec((ns, RET_HEADS, hd, hd), lambda b: (b, 0, 0, 0)),
        ],
        out_shape=[
            jax.ShapeDtypeStruct((nb, 1, RET_WIDTH), F32),
            jax.ShapeDtypeStruct((nb, RET_HEADS, hd, hd), F32),
        ],
        compiler_params=_cparams(("arbitrary",)),
        name="ret_sample",
    )(log_gamma, rest3, rest3, rest3, cos, sin, ret_norm_w, state)


DEC_PAGES = 8
DEC_SLOTS = 4
DEC_GROUP = 2
PAIRS = PAGE_SIZE // 2
PAIR_ROWS = PAIRS * FOX_HEADS
PAIR_WIDTH = 2 * FOX_HEAD_DIM
STREAMS = 2 * FOX_HEADS
SEM_K, SEM_V, SEM_LF = 0, 1, 2


def _pair_view(page_ref):
    even = page_ref[:, 0].reshape(PAIR_ROWS, FOX_HEAD_DIM)
    odd = page_ref[:, 1].reshape(PAIR_ROWS, FOX_HEAD_DIM)
    return jnp.concatenate([even, odd], axis=1).astype(BF16)


def _fox_decode_kernel(pt_ref, q_ref, kn_ref, vn_ref, lfn_ref, ck_hbm, cv_hbm, clf_hbm,
                       o_ref, kbuf, vbuf, lfbuf, sems, m_ref, l_ref, acc_ref, carry_ref,
                       after_ref):
    np_ = DEC_PAGES
    hd = FOX_HEAD_DIM
    nh = FOX_HEADS
    b = pl.program_id(0)
    i = pl.program_id(1)
    steps_per_seq = pl.num_programs(1)
    chunks_per_seq = DEC_SLOTS * steps_per_seq
    total_chunks = pl.num_programs(0) * chunks_per_seq
    first_chunk = (b * steps_per_seq + i) * DEC_SLOTS

    def chunk_copies(chunk, slot):
        seq_id = chunk // chunks_per_seq
        first_page = (chunks_per_seq - 1 - chunk % chunks_per_seq) * np_
        copies = []
        for j in range(np_):
            page = pt_ref[seq_id, first_page + j]
            copies.append(pltpu.make_async_copy(ck_hbm.at[page], kbuf.at[slot, j],
                                                sems.at[slot, SEM_K]))
            copies.append(pltpu.make_async_copy(cv_hbm.at[page], vbuf.at[slot, j],
                                                sems.at[slot, SEM_V]))
            copies.append(pltpu.make_async_copy(clf_hbm.at[page], lfbuf.at[slot, j],
                                                sems.at[slot, SEM_LF]))
        return copies

    def start_chunk(chunk, slot):
        for cp in chunk_copies(chunk, slot):
            cp.start()

    def wait_chunk(chunk, slot):
        for cp in chunk_copies(chunk, slot):
            cp.wait()

    @pl.when((b == 0) & (i == 0))
    def _():
        prow = lax.broadcasted_iota(jnp.int32, (PAGE_SIZE, 2 * PAIR_ROWS), 0)
        lane = lax.broadcasted_iota(jnp.int32, (PAGE_SIZE, 2 * PAIR_ROWS), 1)
        pos = 2 * ((lane % PAIR_ROWS) // nh) + lane // PAIR_ROWS
        after_ref[...] = jnp.where(prow > pos, 1.0, 0.0).astype(BF16)
        for slot in range(DEC_SLOTS):
            start_chunk(slot, slot)

    @pl.when(i == 0)
    def _():
        m_ref[...] = jnp.full((STREAMS, LANES), NEG_INF, F32)
        l_ref[...] = jnp.zeros((STREAMS, LANES), F32)
        acc_ref[...] = jnp.zeros((STREAMS, PAIR_WIDTH), F32)
        carry_ref[...] = jnp.zeros((nh, LANES), F32)

    q = q_ref[...]
    qb = q.astype(BF16)
    zq = jnp.zeros((nh, hd), BF16)
    qw = jnp.concatenate([jnp.concatenate([qb, zq], axis=1),
                          jnp.concatenate([zq, qb], axis=1)], axis=0)
    srow = lax.broadcasted_iota(jnp.int32, (STREAMS, PAIR_ROWS), 0)
    slane = lax.broadcasted_iota(jnp.int32, (STREAMS, PAIR_ROWS), 1)
    own_head = (slane % nh) == (srow % nh)
    cn = lfn_ref[:, 0:1]

    def consume(slot, state):
        m, l, acc, carry = state
        lfs = lfbuf[slot].reshape(np_ * nh, PAGE_SIZE)
        after = after_ref[...]
        suffix = sum(jnp.dot(piece, after, preferred_element_type=F32)
                     for piece in _split3(lfs))
        totals = jnp.sum(lfs, axis=1, keepdims=True)
        head_bias = [None] * np_
        for j in reversed(range(np_)):
            head_bias[j] = carry + cn
            carry = carry + totals[j * nh:(j + 1) * nh]

        def logits(j):
            rows = slice(j * nh, (j + 1) * nh)
            kw = _pair_view(kbuf.at[slot, j])
            s = lax.dot_general(qw, kw, NT_DIMS, preferred_element_type=F32) * FOX_SCALE
            bias = jnp.concatenate([suffix[rows, :PAIR_ROWS] + head_bias[j],
                                    suffix[rows, PAIR_ROWS:] + head_bias[j]], axis=0)
            return jnp.where(own_head, s + bias, NEG_INF)

        def update(m, l, acc, pages, s_pages):
            m_new = m
            for s in s_pages:
                m_new = jnp.maximum(m_new, jnp.max(s, axis=1, keepdims=True))
            alpha = jnp.exp(m - m_new)
            l = alpha * l
            acc = alpha * acc
            for j, s in zip(pages, s_pages):
                p = jnp.exp(s - m_new)
                l = l + jnp.sum(p, axis=1, keepdims=True)
                acc = acc + jnp.dot(p.astype(BF16), _pair_view(vbuf.at[slot, j]),
                                    preferred_element_type=F32)
            return m_new, l, acc

        groups = [list(range(g + DEC_GROUP - 1, g - 1, -1))
                  for g in range(np_ - DEC_GROUP, -1, -DEC_GROUP)]
        s_next = [logits(j) for j in groups[0]]
        for g, pages in enumerate(groups):
            s_cur = s_next
            if g + 1 < len(groups):
                s_next = [logits(j) for j in groups[g + 1]]
            m, l, acc = update(m, l, acc, pages, s_cur)
        return m, l, acc, carry

    state = (m_ref[:, 0:1], l_ref[:, 0:1], acc_ref[...], carry_ref[:, 0:1])
    for slot in range(DEC_SLOTS):
        wait_chunk(first_chunk + slot, slot)
        state = consume(slot, state)
        refill = first_chunk + slot + DEC_SLOTS

        @pl.when(refill < total_chunks)
        def _():
            start_chunk(refill, slot)

    m, l, acc, carry = state
    m_ref[...] = jnp.broadcast_to(m, (STREAMS, LANES))
    l_ref[...] = jnp.broadcast_to(l, (STREAMS, LANES))
    acc_ref[...] = acc
    carry_ref[...] = jnp.broadcast_to(carry, (nh, LANES))

    @pl.when(i == steps_per_seq - 1)
    def _():
        sn = jnp.sum(q * kn_ref[...], axis=1, keepdims=True) * FOX_SCALE
        m_e, m_o = m[:nh], m[nh:]
        m_f = jnp.maximum(jnp.maximum(m_e, m_o), sn)
        w_e = jnp.exp(m_e - m_f)
        w_o = jnp.exp(m_o - m_f)
        p_n = jnp.exp(sn - m_f)
        denom = w_e * l[:nh] + w_o * l[nh:] + p_n
        out = w_e * acc[:nh, :hd] + w_o * acc[nh:, hd:] + p_n * vn_ref[...]
        o_ref[...] = out / denom


def _fox_decode(page_table, q3, kn3, vn3, lfn3, cache_k, cache_v, cache_lf_t):
    nb, n_pages = page_table.shape
    steps_per_seq = n_pages // (DEC_PAGES * DEC_SLOTS)
    nh, hd = FOX_HEADS, FOX_HEAD_DIM
    head_spec = pl.BlockSpec((None, nh, hd), lambda b, i, pt: (b, 0, 0))
    hbm_spec = pl.BlockSpec(memory_space=pl.ANY)
    grid_spec = pltpu.PrefetchScalarGridSpec(
        num_scalar_prefetch=1,
        grid=(nb, steps_per_seq),
        in_specs=[head_spec] * 4 + [hbm_spec] * 3,
        out_specs=head_spec,
        scratch_shapes=[
            pltpu.VMEM((DEC_SLOTS, DEC_PAGES, PAIRS, 2, nh, hd), F32),
            pltpu.VMEM((DEC_SLOTS, DEC_PAGES, PAIRS, 2, nh, hd), F32),
            pltpu.VMEM((DEC_SLOTS, DEC_PAGES, nh, PAGE_SIZE), F32),
            pltpu.SemaphoreType.DMA((DEC_SLOTS, 3)),
            pltpu.VMEM((STREAMS, LANES), F32),
            pltpu.VMEM((STREAMS, LANES), F32),
            pltpu.VMEM((STREAMS, PAIR_WIDTH), F32),
            pltpu.VMEM((nh, LANES), F32),
            pltpu.VMEM((PAGE_SIZE, 2 * PAIR_ROWS), BF16),
        ],
    )
    return pl.pallas_call(
        _fox_decode_kernel,
        grid_spec=grid_spec,
        out_shape=jax.ShapeDtypeStruct((nb, nh, hd), F32),
        compiler_params=_cparams(("arbitrary", "arbitrary")),
        name="fox_decode",
    )(page_table, q3, kn3, vn3, lfn3, cache_k, cache_v, cache_lf_t)


def _merge_kernel(x_ref, oa_ref, za_ref, ob_ref, zb_ref, ga_ref, gb_ref,
                  wua_ref, wub_ref, wo_ref, fnw_ref, y_ref):
    a = (oa_ref[...] * jax.nn.silu(za_ref[...].astype(F32))).astype(BF16)
    b = (ob_ref[...] * jax.nn.silu(zb_ref[...].astype(F32))).astype(BF16)
    ua = jnp.dot(a, wua_ref[...], preferred_element_type=F32)
    ub = jnp.dot(b, wub_ref[...], preferred_element_type=F32)
    merged = (jax.nn.sigmoid(ga_ref[...].astype(F32)) * ua
              + jax.nn.sigmoid(gb_ref[...].astype(F32)) * ub)
    out = x_ref[...] + jnp.dot(merged.astype(BF16), wo_ref[...], preferred_element_type=F32)
    ms = jnp.mean(out * out, axis=-1, keepdims=True)
    y_ref[...] = out * lax.rsqrt(ms + EPS) * fnw_ref[...]


def _merge(x, o_a, o_b, rest, w_up_a, w_up_b, w_o, final_norm_w, tm):
    m = x.shape[0]
    const = dict(pipeline_mode=pl.Buffered(1))
    return pl.pallas_call(
        _merge_kernel,
        grid=(m // tm,),
        in_specs=[
            pl.BlockSpec((tm, D_MODEL), lambda i: (i, 0)),
            pl.BlockSpec((tm, FOX_WIDTH), lambda i: (i, 0)),
            pl.BlockSpec((tm, FOX_WIDTH), lambda i: (i, REST_ZA // FOX_WIDTH)),
            pl.BlockSpec((tm, RET_WIDTH), lambda i: (i, 0)),
            pl.BlockSpec((tm, RET_WIDTH), lambda i: (i, REST_ZB // RET_WIDTH)),
            pl.BlockSpec((tm, D_MODEL), lambda i: (i, REST_GA // D_MODEL)),
            pl.BlockSpec((tm, D_MODEL), lambda i: (i, REST_GB // D_MODEL)),
            pl.BlockSpec((FOX_WIDTH, D_MODEL), lambda i: (0, 0), **const),
            pl.BlockSpec((RET_WIDTH, D_MODEL), lambda i: (0, 0), **const),
            pl.BlockSpec((D_MODEL, D_MODEL), lambda i: (0, 0), **const),
            pl.BlockSpec((1, D_MODEL), lambda i: (0, 0)),
        ],
        out_specs=pl.BlockSpec((tm, D_MODEL), lambda i: (i, 0)),
        out_shape=jax.ShapeDtypeStruct((m, D_MODEL), F32),
        compiler_params=_cparams(("arbitrary",)),
        name="merge",
    )(x, o_a, rest, o_b, rest, rest, rest, w_up_a, w_up_b, w_o, final_norm_w)


def _rope_tables(pos):
    half = RET_HEAD_DIM // 2
    inv = 1.0 / (ROPE_BASE ** (jnp.arange(half, dtype=F32) / half))
    ang = pos.astype(F32)[:, None] * inv[None, :]
    return jnp.cos(ang), jnp.sin(ang)


def _layer(xp, xs, cache_k, cache_v, cache_logf, state_ret, page_table,
           norm_w, w_in, b_f, ret_norm_w, w_up_a, w_up_b, w_o, final_norm_w):
    batch, seq, _ = xp.shape
    nb = xs.shape[0]
    n_pool = cache_k.shape[0]
    past = page_table.shape[1] * PAGE_SIZE
    nh, hd, fw, rw = FOX_HEADS, FOX_HEAD_DIM, FOX_WIDTH, RET_WIDTH

    wt = w_in.T
    bf_pad = jnp.pad(b_f, (0, F_ROWS - nh)).reshape(1, F_ROWS)
    wua, wub, wo = w_up_a.astype(BF16), w_up_b.astype(BF16), w_o.astype(BF16)
    nw = norm_w.reshape(1, D_MODEL)
    rnw = ret_norm_w.reshape(1, rw)
    fnw = final_norm_w.reshape(1, D_MODEL)
    log_gamma = jnp.log(1.0 - 2.0 ** (-5.0 - jnp.arange(RET_HEADS, dtype=F32)))
    cos_p, sin_p = _rope_tables(jnp.arange(seq, dtype=jnp.int32))
    cos_s, sin_s = _rope_tables(past + jnp.arange(1, dtype=jnp.int32))

    x2 = xp.reshape(batch * seq, D_MODEL)
    k_p, v_p, rest_p, f_p = _inproj(x2, nw, wt, tm=1024, rest_dtype=BF16)
    logf_p, c_p = _logf_cumsum(f_p, bf_pad, batch, seq)
    o_a = _fox_prompt(rest_p, k_p, v_p, c_p, batch, seq)
    o_b, st_p = _ret_prompt(log_gamma, rest_p, cos_p, sin_p, rnw, batch, seq)
    y_p = _merge(x2, o_a, o_b, rest_p, wua, wub, wo, fnw, tm=256)

    xs2 = xs.reshape(nb, D_MODEL)
    k_s, v_s, rest_s, f_s = _inproj(xs2, nw, wt, tm=nb, rest_dtype=F32)
    logf_s = _logf(f_s, bf_pad)
    rest_s3 = rest_s.reshape(nb, 1, REST_WIDTH)
    q_s = rest_s[:, REST_QA:REST_QA + fw].reshape(nb, nh, hd)
    lfn = jnp.broadcast_to(logf_s[:, :, None], (nb, nh, hd))
    o_a_s = _fox_decode(
        page_table, q_s, k_s.reshape(nb, nh, hd), v_s.reshape(nb, nh, hd), lfn,
        cache_k.reshape(n_pool, PAIRS, 2, nh, hd), cache_v.reshape(n_pool, PAIRS, 2, nh, hd),
        jnp.swapaxes(cache_logf, 1, 2))
    o_b_s, st_s = _ret_sample(log_gamma, rest_s3, cos_s, sin_s, rnw, state_ret)
    y_s = _merge(xs2, o_a_s.reshape(nb, fw), o_b_s.reshape(nb, rw), rest_s,
                 wua, wub, wo, fnw, tm=nb)

    hshape = (nh, hd)
    return (y_p.reshape(batch, seq, D_MODEL), y_s.reshape(nb, 1, D_MODEL),
            k_p.reshape(batch, seq, *hshape), v_p.reshape(batch, seq, *hshape),
            logf_p.reshape(batch, seq, nh), st_p,
            k_s.reshape(nb, 1, *hshape), v_s.reshape(nb, 1, *hshape),
            logf_s.reshape(nb, 1, nh), st_s)


def kernel(x_prompt, x_sample, cache_k, cache_v, cache_logf, state_ret, page_table,
           norm_w, w_in, b_f, ret_norm_w, w_up_a, w_up_b, w_o, final_norm_w):
    depth = w_in.shape[0]
    assert depth == 1 and x_sample.shape[1] == 1
    outs = _layer(x_prompt, x_sample, cache_k[0], cache_v[0], cache_logf[0], state_ret[0],
                  page_table, norm_w[0], w_in[0], b_f[0], ret_norm_w[0], w_up_a[0],
                  w_up_b[0], w_o[0], final_norm_w)
    y_p, y_s, k_p, v_p, lf_p, st_p, k_s, v_s, lf_s, st_s = outs
    stack = lambda a: a[None]
    return (y_p, y_s, stack(k_p), stack(v_p), stack(lf_p), stack(st_p),
            stack(k_s), stack(v_s), stack(lf_s), stack(st_s))
```

```python
import functools
import itertools
import math

import jax
import jax.numpy as jnp
from jax import lax
from jax.experimental import pallas as pl
from jax.experimental.pallas import tpu as pltpu

F32 = jnp.float32
BF16 = jnp.bfloat16

D_MODEL = 2048
FOX_HEADS = 8
FOX_HEAD_DIM = 128
FOX_WIDTH = FOX_HEADS * FOX_HEAD_DIM
RET_HEADS = 4
RET_HEAD_DIM = 256
RET_WIDTH = RET_HEADS * RET_HEAD_DIM
RET_CHUNK = 128
PAGE_SIZE = 128
ROPE_BASE = 10000.0
EPS = 1e-6
NEG_INF = -1e30
FOX_SCALE = FOX_HEAD_DIM ** -0.5
RET_K_SCALE = RET_HEAD_DIM ** -0.5
LOG2E = math.log2(math.e)

LANES = 128
SUBLANES = 8
VMEM_LIMIT = 56 * 1024 * 1024

REST_QA, REST_ZA, REST_QB, REST_KB, REST_VB, REST_ZB, REST_GA, REST_GB = (
    0, 1024, 2048, 3072, 4096, 5120, 6144, 8192)
REST_WIDTH = 10240
PROJ_TN = 1024
N_KV_TILES = FOX_WIDTH // PROJ_TN
F_ROWS = LANES
W_STREAMS = 1

NT_DIMS = (((1,), (1,)), ((), ()))
TN_DIMS = (((0,), (0,)), ((), ()))


def _cparams(sem, vmem=VMEM_LIMIT):
    return pltpu.CompilerParams(dimension_semantics=sem, vmem_limit_bytes=vmem)


def _inproj_kernel(x_ref, nw_ref, *refs, weights_as_lhs):
    wt_refs = refs[:W_STREAMS]
    wf_ref, k_ref, v_ref, rest_ref, f_ref, h_ref = refs[W_STREAMS:]
    j = pl.program_id(1)
    m = x_ref.shape[0]

    def project(w_refs):
        w = jnp.concatenate([r[...] for r in w_refs], axis=0).astype(BF16)
        if weights_as_lhs:
            out_t = lax.dot_general(w, h_ref[...], NT_DIMS, preferred_element_type=F32)
            return out_t.T[:m]
        return lax.dot_general(h_ref[...], w, NT_DIMS, preferred_element_type=F32)

    @pl.when(j == 0)
    def _():
        x = x_ref[...]
        ms = jnp.mean(x * x, axis=-1, keepdims=True)
        h = (x * lax.rsqrt(ms + EPS) * nw_ref[...]).astype(BF16)
        if h_ref.shape[0] != m:
            h_ref[...] = jnp.zeros(h_ref.shape, BF16)
        h_ref[:m, :] = h
        f_ref[...] = project([wf_ref])

    acc = project(wt_refs)

    @pl.when(j < N_KV_TILES)
    def _():
        k_ref[...] = acc

    @pl.when((j >= N_KV_TILES) & (j < 2 * N_KV_TILES))
    def _():
        v_ref[...] = acc

    @pl.when(j >= 2 * N_KV_TILES)
    def _():
        rest_ref[...] = acc.astype(rest_ref.dtype)


def _wt_row_offset(j):
    t, n, fw = PROJ_TN, N_KV_TILES, FOX_WIDTH
    after_f = 4 * fw + FOX_HEADS
    return jnp.where(
        j < n, fw + j * t,
        jnp.where(j < 2 * n, 2 * fw + (j - n) * t,
                  jnp.where(j < 3 * n, (j - 2 * n) * t,
                            jnp.where(j < 4 * n, 3 * fw + (j - 3 * n) * t,
                                      after_f + (j - 4 * n) * t))))


def _inproj(x, norm_w, wt, tm, rest_dtype):
    m = x.shape[0]
    n_tiles = (wt.shape[0] - FOX_HEADS) // PROJ_TN
    kv_last = N_KV_TILES - 1
    weights_as_lhs = tm < LANES
    rows = PROJ_TN // W_STREAMS
    once = dict(pipeline_mode=pl.Buffered(1))

    def wt_map(s, i, j):
        return (pl.multiple_of(_wt_row_offset(j) + s * rows, SUBLANES), 0)

    return pl.pallas_call(
        functools.partial(_inproj_kernel, weights_as_lhs=weights_as_lhs),
        grid=(m // tm, n_tiles),
        in_specs=[
            pl.BlockSpec((tm, D_MODEL), lambda i, j: (i, 0), **once),
            pl.BlockSpec((1, D_MODEL), lambda i, j: (0, 0)),
            *[pl.BlockSpec((pl.Element(rows), pl.Element(D_MODEL)), functools.partial(wt_map, s))
              for s in range(W_STREAMS)],
            pl.BlockSpec((F_ROWS, D_MODEL), lambda i, j: (4 * FOX_WIDTH // F_ROWS, 0), **once),
        ],
        out_specs=[
            pl.BlockSpec((tm, PROJ_TN), lambda i, j: (i, jnp.minimum(j, kv_last)), **once),
            pl.BlockSpec((tm, PROJ_TN),
                         lambda i, j: (i, jnp.clip(j - N_KV_TILES, 0, kv_last)), **once),
            pl.BlockSpec((tm, PROJ_TN),
                         lambda i, j: (i, jnp.maximum(j - 2 * N_KV_TILES, 0))),
            pl.BlockSpec((tm, F_ROWS), lambda i, j: (i, 0)),
        ],
        out_shape=[
            jax.ShapeDtypeStruct((m, FOX_WIDTH), F32),
            jax.ShapeDtypeStruct((m, FOX_WIDTH), F32),
            jax.ShapeDtypeStruct((m, REST_WIDTH), rest_dtype),
            jax.ShapeDtypeStruct((m, F_ROWS), F32),
        ],
        scratch_shapes=[pltpu.VMEM((max(tm, LANES), D_MODEL), BF16)],
        compiler_params=_cparams(("arbitrary", "arbitrary")),
        name="inproj",
    )(x, norm_w, *([wt] * W_STREAMS), wt)


def _logf_kernel(f_ref, bf_ref, logf_ref):
    lf = jax.nn.log_sigmoid(f_ref[...] + bf_ref[...])
    logf_ref[...] = lf[:, :FOX_HEADS]


def _logf(f, bf_pad):
    m = f.shape[0]
    return pl.pallas_call(
        _logf_kernel,
        grid=(1,),
        in_specs=[pl.BlockSpec((m, F_ROWS), lambda i: (0, 0)),
                  pl.BlockSpec((1, F_ROWS), lambda i: (0, 0))],
        out_specs=pl.BlockSpec((m, FOX_HEADS), lambda i: (0, 0)),
        out_shape=jax.ShapeDtypeStruct((m, FOX_HEADS), F32),
        name="logf_sample",
    )(f, bf_pad)


def _logf_cumsum_kernel(f_ref, bf_ref, logf_ref, c_ref):
    seq = f_ref.shape[0]
    lf = jax.nn.log_sigmoid(f_ref[...] + bf_ref[...])
    logf_ref[...] = lf[:, :FOX_HEADS]
    row = lax.broadcasted_iota(jnp.int32, (LANES, LANES), 0)
    col = lax.broadcasted_iota(jnp.int32, (LANES, LANES), 1)
    lower = (col <= row).astype(F32)
    carry = jnp.zeros((1, LANES), F32)
    for blk in range(seq // LANES):
        rows = slice(blk * LANES, (blk + 1) * LANES)
        cs = jnp.dot(lower, lf[rows, :], preferred_element_type=F32,
                     precision=lax.Precision.HIGHEST) + carry
        c_ref[rows, :] = cs
        carry = cs[LANES - 1:LANES, :]


def _logf_cumsum(f, bf_pad, batch, seq):
    return pl.pallas_call(
        _logf_cumsum_kernel,
        grid=(batch,),
        in_specs=[pl.BlockSpec((seq, F_ROWS), lambda b: (b, 0)),
                  pl.BlockSpec((1, F_ROWS), lambda b: (0, 0))],
        out_specs=[pl.BlockSpec((seq, FOX_HEADS), lambda b: (b, 0)),
                   pl.BlockSpec((seq, LANES), lambda b: (b, 0))],
        out_shape=[jax.ShapeDtypeStruct((batch * seq, FOX_HEADS), F32),
                   jax.ShapeDtypeStruct((batch * seq, LANES), F32)],
        compiler_params=_cparams(("arbitrary",)),
        name="logf_cumsum",
    )(f, bf_pad)


FOX_TQ = 1024
FOX_TK = 512
FOX_HPS = 4
N_BIAS_FEATURES = 3


def _split3(x):
    hi = x.astype(BF16)
    r1 = x - hi.astype(F32)
    mid = r1.astype(BF16)
    lo = (r1 - mid.astype(F32)).astype(BF16)
    return hi, mid, lo


def _fox_prompt_kernel(q_ref, k_ref, v_ref, c_ref, o_ref, kx_ref, vt_ref):
    hg = pl.program_id(1)
    qi = pl.program_id(2)
    hd = FOX_HEAD_DIM
    seq = k_ref.shape[0]
    heads = range(FOX_HPS)

    @pl.when(qi == 0)
    def _():
        lane = lax.broadcasted_iota(jnp.int32, (seq, LANES), 1)
        for hh in heads:
            cols = slice(hh * hd, (hh + 1) * hd)
            c_h = jnp.sum(jnp.where(lane == hg * FOX_HPS + hh, c_ref[...], 0.0),
                          axis=1, keepdims=True)
            hi, mid, lo = (piece.astype(F32) for piece in _split3(
                jnp.broadcast_to(c_h * (1.0 / FOX_SCALE), (seq, LANES))))
            feat = jnp.where(lane == 0, hi, jnp.where(lane == 1, mid,
                                                      jnp.where(lane == 2, lo, 0.0)))
            kx_ref[hh, :, :hd] = k_ref[:, cols].astype(BF16)
            kx_ref[hh, :, hd:] = feat.astype(BF16)
            vt_ref[hh, :hd, :] = v_ref[:, cols].T.astype(BF16)
            vt_ref[hh, hd:, :] = jnp.ones((hd, seq), BF16)

    qlane = lax.broadcasted_iota(jnp.int32, (FOX_TQ, LANES), 1)
    qfeat = jnp.where(qlane < N_BIAS_FEATURES, -1.0, 0.0).astype(BF16)
    qx = [jnp.concatenate([q_ref[:, hh * hd:(hh + 1) * hd].astype(BF16), qfeat], axis=1)
          for hh in heads]

    def block(kbi, carry, diag_offset=None):
        start = pl.multiple_of(kbi * FOX_TK, FOX_TK)
        out = []
        for hh in heads:
            m, acc = carry[hh]
            kx = kx_ref[hh, pl.ds(start, FOX_TK), :]
            s = lax.dot_general(kx, qx[hh], NT_DIMS, preferred_element_type=F32)
            s = s * (FOX_SCALE * LOG2E)
            if diag_offset is not None:
                kpos = lax.broadcasted_iota(jnp.int32, (FOX_TK, FOX_TQ), 0) + diag_offset
                qpos = lax.broadcasted_iota(jnp.int32, (FOX_TK, FOX_TQ), 1)
                s = jnp.where(kpos <= qpos, s, NEG_INF)
            m_new = jnp.maximum(m, jnp.max(s, axis=0, keepdims=True))
            alpha = jnp.exp2(m - m_new)
            p = jnp.exp2(s - m_new)
            acc = alpha * acc + jnp.dot(vt_ref[hh, :, pl.ds(start, FOX_TK)], p.astype(BF16),
                                        preferred_element_type=F32)
            out.append((m_new, acc))
        return tuple(out)

    init = tuple((jnp.full((1, FOX_TQ), NEG_INF, F32), jnp.zeros((2 * hd, FOX_TQ), F32))
                 for _ in heads)
    per_q = FOX_TQ // FOX_TK
    final = lax.fori_loop(0, qi * per_q, block, init)
    for t in range(per_q):
        final = block(qi * per_q + t, final, diag_offset=t * FOX_TK)
    for hh in heads:
        acc = final[hh][1]
        o_ref[:, hh * hd:(hh + 1) * hd] = (acc[:hd] / acc[hd:]).T


def _fox_prompt(rest, k, v, c, batch, seq):
    nq = seq // FOX_TQ
    hw = FOX_HPS * FOX_HEAD_DIM
    return pl.pallas_call(
        _fox_prompt_kernel,
        grid=(batch, FOX_HEADS // FOX_HPS, nq),
        in_specs=[
            pl.BlockSpec((FOX_TQ, hw), lambda b, g, qi: (b * nq + qi, REST_QA // hw + g)),
            pl.BlockSpec((seq, hw), lambda b, g, qi: (b, g)),
            pl.BlockSpec((seq, hw), lambda b, g, qi: (b, g)),
            pl.BlockSpec((seq, LANES), lambda b, g, qi: (b, 0)),
        ],
        out_specs=pl.BlockSpec((FOX_TQ, hw), lambda b, g, qi: (b * nq + qi, g)),
        out_shape=jax.ShapeDtypeStruct((batch * seq, FOX_WIDTH), F32),
        scratch_shapes=[pltpu.VMEM((FOX_HPS, seq, 2 * FOX_HEAD_DIM), BF16),
                        pltpu.VMEM((FOX_HPS, 2 * FOX_HEAD_DIM, seq), BF16)],
        compiler_params=_cparams(("arbitrary", "arbitrary", "arbitrary")),
        name="fox_prompt",
    )(rest, k, v, c)


def _rotary(x, cos, sin):
    half = RET_HEAD_DIM // 2
    x1 = x[:, :half]
    x2 = x[:, half:]
    return jnp.concatenate([x1 * cos - x2 * sin, x2 * cos + x1 * sin], axis=-1)


def _head_rmsnorm(o, w):
    return o * lax.rsqrt(jnp.mean(o * o, axis=-1, keepdims=True) + EPS) * w


def _ret_prompt_kernel(lg_ref, q_ref, k_ref, v_ref, cos_ref, sin_ref, w_ref,
                       o_ref, st_ref):
    h = pl.program_id(1)
    lg = lg_ref[h]
    n_chunks = q_ref.shape[0] // RET_CHUNK
    ri = lax.broadcasted_iota(jnp.int32, (RET_CHUNK, RET_CHUNK), 0)
    ci = lax.broadcasted_iota(jnp.int32, (RET_CHUNK, RET_CHUNK), 1)
    diff = (ri - ci).astype(F32)
    decay = jnp.where(diff >= 0, jnp.exp(lg * jnp.maximum(diff, 0.0)), 0.0)
    pos = lax.broadcasted_iota(jnp.int32, (RET_CHUNK, RET_HEAD_DIM), 0).astype(F32)
    cross = jnp.exp(lg * (pos + 1.0))
    kdec = jnp.exp(lg * (RET_CHUNK - 1.0 - pos))
    g_chunk = jnp.exp(jnp.full((1, RET_HEAD_DIM), lg * RET_CHUNK, F32))
    w = w_ref[...]

    st_ref[0, 0] = jnp.zeros((RET_HEAD_DIM, RET_HEAD_DIM), F32)

    def chunk(c, _):
        r0 = pl.multiple_of(c * RET_CHUNK, RET_CHUNK)
        rows = pl.ds(r0, RET_CHUNK)
        cos = cos_ref[rows, :]
        sin = sin_ref[rows, :]
        qr = _rotary(q_ref[rows, :].astype(F32), cos, sin)
        kr = _rotary(k_ref[rows, :].astype(F32), cos, sin) * RET_K_SCALE
        qb = qr.astype(BF16)
        kb = kr.astype(BF16)
        vb = v_ref[rows, :].astype(BF16)
        st = st_ref[0, 0]
        scores = lax.dot_general(qb, kb, NT_DIMS, preferred_element_type=F32) * decay
        o = jnp.dot(scores.astype(BF16), vb, preferred_element_type=F32)
        o = o + cross * jnp.dot(qb, st.astype(BF16), preferred_element_type=F32)
        kd = (kr * kdec).astype(BF16)
        st_ref[0, 0] = g_chunk * st + lax.dot_general(kd, vb, TN_DIMS,
                                                      preferred_element_type=F32)
        o_ref[rows, :] = _head_rmsnorm(o, w)
        return 0

    lax.fori_loop(0, n_chunks, chunk, 0, unroll=4)


def _ret_prompt(log_gamma, rest, cos, sin, ret_norm_w, batch, seq):
    hd = RET_HEAD_DIM
    return pl.pallas_call(
        _ret_prompt_kernel,
        grid=(batch, RET_HEADS),
        in_specs=[
            pl.BlockSpec(memory_space=pltpu.SMEM),
            pl.BlockSpec((seq, hd), lambda b, h: (b, REST_QB // hd + h)),
            pl.BlockSpec((seq, hd), lambda b, h: (b, REST_KB // hd + h)),
            pl.BlockSpec((seq, hd), lambda b, h: (b, REST_VB // hd + h)),
            pl.BlockSpec((seq, hd // 2), lambda b, h: (0, 0)),
            pl.BlockSpec((seq, hd // 2), lambda b, h: (0, 0)),
            pl.BlockSpec((1, hd), lambda b, h: (0, h)),
        ],
        out_specs=[
            pl.BlockSpec((seq, hd), lambda b, h: (b, h)),
            pl.BlockSpec((1, 1, hd, hd), lambda b, h: (b, h, 0, 0)),
        ],
        out_shape=[
            jax.ShapeDtypeStruct((batch * seq, RET_WIDTH), F32),
            jax.ShapeDtypeStruct((batch, RET_HEADS, hd, hd), F32),
        ],
        compiler_params=_cparams(("arbitrary", "arbitrary")),
        name="ret_prompt",
    )(log_gamma, rest, rest, rest, cos, sin, ret_norm_w)


RET_SAMPLE_SEQS = 2


def _ret_sample_kernel(lg_ref, q_ref, k_ref, v_ref, cos_ref, sin_ref, w_ref, st_ref,
                       o_ref, nst_ref):
    hd = RET_HEAD_DIM
    cos = cos_ref[...]
    sin = sin_ref[...]
    first_row = lax.broadcasted_iota(jnp.int32, (16, hd), 0) == 0
    for s, h in itertools.product(range(q_ref.shape[0]), range(RET_HEADS)):
        cols = slice(h * hd, (h + 1) * hd)
        gamma = jnp.exp(jnp.full((1, hd), lg_ref[h], F32))
        qr = _rotary(q_ref[s, :, cols], cos, sin)
        kr = _rotary(k_ref[s, :, cols], cos, sin) * RET_K_SCALE
        vh = v_ref[s, :, cols]
        st = st_ref[s, h]
        q16 = jnp.broadcast_to(qr, (16, hd)).astype(BF16)
        qs = jnp.dot(q16, st.astype(BF16), preferred_element_type=F32)[0:1]
        score = jnp.sum(qr * kr, axis=-1, keepdims=True)
        o = score * vh + gamma * qs
        k16 = jnp.where(first_row, jnp.broadcast_to(kr, (16, hd)), 0.0).astype(BF16)
        v16 = jnp.broadcast_to(vh, (16, hd)).astype(BF16)
        nst_ref[s, h] = gamma * st + lax.dot_general(k16, v16, TN_DIMS,
                                                     preferred_element_type=F32)
        o_ref[s, :, cols] = _head_rmsnorm(o, w_ref[:, cols])


def _ret_sample(log_gamma, rest3, cos, sin, ret_norm_w, state):
    nb = rest3.shape[0]
    hd = RET_HEAD_DIM
    ns = RET_SAMPLE_SEQS
    return pl.pallas_call(
        _ret_sample_kernel,
        grid=(nb // ns,),
        in_specs=[
            pl.BlockSpec(memory_space=pltpu.SMEM),
            pl.BlockSpec((ns, 1, RET_WIDTH), lambda b: (b, 0, REST_QB // RET_WIDTH)),
            pl.BlockSpec((ns, 1, RET_WIDTH), lambda b: (b, 0, REST_KB // RET_WIDTH)),
            pl.BlockSpec((ns, 1, RET_WIDTH), lambda b: (b, 0, REST_VB // RET_WIDTH)),
            pl.BlockSpec((1, hd // 2), lambda b: (0, 0)),
            pl.BlockSpec((1, hd // 2), lambda b: (0, 0)),
            pl.BlockSpec((1, RET_WIDTH), lambda b: (0, 0)),
            pl.BlockSpec((ns, RET_HEADS, hd, hd), lambda b: (b, 0, 0, 0)),
        ],
        out_specs=[
            pl.BlockSpec((ns, 1, RET_WIDTH), lambda b: (b, 0, 0)),
            pl.BlockSpec((ns, RET_HEADS, hd, hd), lambda b: (b, 0, 0, 0)),
        ],
        out_shape=[
            jax.ShapeDtypeStruct((nb, 1, RET_WIDTH), F32),
            jax.ShapeDtypeStruct((nb, RET_HEADS, hd, hd), F32),
        ],
        compiler_params=_cparams(("arbitrary",)),
        name="ret_sample",
    )(log_gamma, rest3, rest3, rest3, cos, sin, ret_norm_w, state)


DEC_PAGES = 8
DEC_SLOTS = 4
DEC_GROUP = 2
PAIRS = PAGE_SIZE // 2
PAIR_ROWS = PAIRS * FOX_HEADS
PAIR_WIDTH = 2 * FOX_HEAD_DIM
STREAMS = 2 * FOX_HEADS
SEM_K, SEM_V, SEM_LF = 0, 1, 2


def _pair_view(page_ref):
    even = page_ref[:, 0].reshape(PAIR_ROWS, FOX_HEAD_DIM)
    odd = page_ref[:, 1].reshape(PAIR_ROWS, FOX_HEAD_DIM)
    return jnp.concatenate([even, odd], axis=1).astype(BF16)


def _fox_decode_kernel(pt_ref, q_ref, kn_ref, vn_ref, lfn_ref, ck_hbm, cv_hbm, clf_hbm,
                       o_ref, kbuf, vbuf, lfbuf, sems, m_ref, l_ref, acc_ref, carry_ref,
                       after_ref):
    np_ = DEC_PAGES
    hd = FOX_HEAD_DIM
    nh = FOX_HEADS
    b = pl.program_id(0)
    i = pl.program_id(1)
    steps_per_seq = pl.num_programs(1)
    chunks_per_seq = DEC_SLOTS * steps_per_seq
    total_chunks = pl.num_programs(0) * chunks_per_seq
    first_chunk = (b * steps_per_seq + i) * DEC_SLOTS

    def chunk_copies(chunk, slot):
        seq_id = chunk // chunks_per_seq
        first_page = (chunks_per_seq - 1 - chunk % chunks_per_seq) * np_
        copies = []
        for j in range(np_):
            page = pt_ref[seq_id, first_page + j]
            copies.append(pltpu.make_async_copy(ck_hbm.at[page], kbuf.at[slot, j],
                                                sems.at[slot, SEM_K]))
            copies.append(pltpu.make_async_copy(cv_hbm.at[page], vbuf.at[slot, j],
                                                sems.at[slot, SEM_V]))
            copies.append(pltpu.make_async_copy(clf_hbm.at[page], lfbuf.at[slot, j],
                                                sems.at[slot, SEM_LF]))
        return copies

    def start_chunk(chunk, slot):
        for cp in chunk_copies(chunk, slot):
            cp.start()

    def wait_chunk(chunk, slot):
        for cp in chunk_copies(chunk, slot):
            cp.wait()

    @pl.when((b == 0) & (i == 0))
    def _():
        prow = lax.broadcasted_iota(jnp.int32, (PAGE_SIZE, 2 * PAIR_ROWS), 0)
        lane = lax.broadcasted_iota(jnp.int32, (PAGE_SIZE, 2 * PAIR_ROWS), 1)
        pos = 2 * ((lane % PAIR_ROWS) // nh) + lane // PAIR_ROWS
        after_ref[...] = jnp.where(prow > pos, 1.0, 0.0).astype(BF16)
        for slot in range(DEC_SLOTS):
            start_chunk(slot, slot)

    @pl.when(i == 0)
    def _():
        m_ref[...] = jnp.full((STREAMS, LANES), NEG_INF, F32)
        l_ref[...] = jnp.zeros((STREAMS, LANES), F32)
        acc_ref[...] = jnp.zeros((STREAMS, PAIR_WIDTH), F32)
        carry_ref[...] = jnp.zeros((nh, LANES), F32)

    q = q_ref[...]
    qb = q.astype(BF16)
    zq = jnp.zeros((nh, hd), BF16)
    qw = jnp.concatenate([jnp.concatenate([qb, zq], axis=1),
                          jnp.concatenate([zq, qb], axis=1)], axis=0)
    srow = lax.broadcasted_iota(jnp.int32, (STREAMS, PAIR_ROWS), 0)
    slane = lax.broadcasted_iota(jnp.int32, (STREAMS, PAIR_ROWS), 1)
    own_head = (slane % nh) == (srow % nh)
    cn = lfn_ref[:, 0:1]

    def consume(slot, state):
        m, l, acc, carry = state
        lfs = lfbuf[slot].reshape(np_ * nh, PAGE_SIZE)
        after = after_ref[...]
        suffix = sum(jnp.dot(piece, after, preferred_element_type=F32)
                     for piece in _split3(lfs))
        totals = jnp.sum(lfs, axis=1, keepdims=True)
        head_bias = [None] * np_
        for j in reversed(range(np_)):
            head_bias[j] = carry + cn
            carry = carry + totals[j * nh:(j + 1) * nh]

        def logits(j):
            rows = slice(j * nh, (j + 1) * nh)
            kw = _pair_view(kbuf.at[slot, j])
            s = lax.dot_general(qw, kw, NT_DIMS, preferred_element_type=F32) * FOX_SCALE
            bias = jnp.concatenate([suffix[rows, :PAIR_ROWS] + head_bias[j],
                                    suffix[rows, PAIR_ROWS:] + head_bias[j]], axis=0)
            return jnp.where(own_head, s + bias, NEG_INF)

        def update(m, l, acc, pages, s_pages):
            m_new = m
            for s in s_pages:
                m_new = jnp.maximum(m_new, jnp.max(s, axis=1, keepdims=True))
            alpha = jnp.exp(m - m_new)
            l = alpha * l
            acc = alpha * acc
            for j, s in zip(pages, s_pages):
                p = jnp.exp(s - m_new)
                l = l + jnp.sum(p, axis=1, keepdims=True)
                acc = acc + jnp.dot(p.astype(BF16), _pair_view(vbuf.at[slot, j]),
                                    preferred_element_type=F32)
            return m_new, l, acc

        groups = [list(range(g + DEC_GROUP - 1, g - 1, -1))
                  for g in range(np_ - DEC_GROUP, -1, -DEC_GROUP)]
        s_next = [logits(j) for j in groups[0]]
        for g, pages in enumerate(groups):
            s_cur = s_next
            if g + 1 < len(groups):
                s_next = [logits(j) for j in groups[g + 1]]
            m, l, acc = update(m, l, acc, pages, s_cur)
        return m, l, acc, carry

    state = (m_ref[:, 0:1], l_ref[:, 0:1], acc_ref[...], carry_ref[:, 0:1])
    for slot in range(DEC_SLOTS):
        wait_chunk(first_chunk + slot, slot)
        state = consume(slot, state)
        refill = first_chunk + slot + DEC_SLOTS

        @pl.when(refill < total_chunks)
        def _():
            start_chunk(refill, slot)

    m, l, acc, carry = state
    m_ref[...] = jnp.broadcast_to(m, (STREAMS, LANES))
    l_ref[...] = jnp.broadcast_to(l, (STREAMS, LANES))
    acc_ref[...] = acc
    carry_ref[...] = jnp.broadcast_to(carry, (nh, LANES))

    @pl.when(i == steps_per_seq - 1)
    def _():
        sn = jnp.sum(q * kn_ref[...], axis=1, keepdims=True) * FOX_SCALE
        m_e, m_o = m[:nh], m[nh:]
        m_f = jnp.maximum(jnp.maximum(m_e, m_o), sn)
        w_e = jnp.exp(m_e - m_f)
        w_o = jnp.exp(m_o - m_f)
        p_n = jnp.exp(sn - m_f)
        denom = w_e * l[:nh] + w_o * l[nh:] + p_n
        out = w_e * acc[:nh, :hd] + w_o * acc[nh:, hd:] + p_n * vn_ref[...]
        o_ref[...] = out / denom


def _fox_decode(page_table, q3, kn3, vn3, lfn3, cache_k, cache_v, cache_lf_t):
    nb, n_pages = page_table.shape
    steps_per_seq = n_pages // (DEC_PAGES * DEC_SLOTS)
    nh, hd = FOX_HEADS, FOX_HEAD_DIM
    head_spec = pl.BlockSpec((None, nh, hd), lambda b, i, pt: (b, 0, 0))
    hbm_spec = pl.BlockSpec(memory_space=pl.ANY)
    grid_spec = pltpu.PrefetchScalarGridSpec(
        num_scalar_prefetch=1,
        grid=(nb, steps_per_seq),
        in_specs=[head_spec] * 4 + [hbm_spec] * 3,
        out_specs=head_spec,
        scratch_shapes=[
            pltpu.VMEM((DEC_SLOTS, DEC_PAGES, PAIRS, 2, nh, hd), F32),
            pltpu.VMEM((DEC_SLOTS, DEC_PAGES, PAIRS, 2, nh, hd), F32),
            pltpu.VMEM((DEC_SLOTS, DEC_PAGES, nh, PAGE_SIZE), F32),
            pltpu.SemaphoreType.DMA((DEC_SLOTS, 3)),
            pltpu.VMEM((STREAMS, LANES), F32),
            pltpu.VMEM((STREAMS, LANES), F32),
            pltpu.VMEM((STREAMS, PAIR_WIDTH), F32),
            pltpu.VMEM((nh, LANES), F32),
            pltpu.VMEM((PAGE_SIZE, 2 * PAIR_ROWS), BF16),
        ],
    )
    return pl.pallas_call(
        _fox_decode_kernel,
        grid_spec=grid_spec,
        out_shape=jax.ShapeDtypeStruct((nb, nh, hd), F32),
        compiler_params=_cparams(("arbitrary", "arbitrary")),
        name="fox_decode",
    )(page_table, q3, kn3, vn3, lfn3, cache_k, cache_v, cache_lf_t)


def _merge_kernel(x_ref, oa_ref, za_ref, ob_ref, zb_ref, ga_ref, gb_ref,
                  wua_ref, wub_ref, wo_ref, fnw_ref, y_ref):
    a = (oa_ref[...] * jax.nn.silu(za_ref[...].astype(F32))).astype(BF16)
    b = (ob_ref[...] * jax.nn.silu(zb_ref[...].astype(F32))).astype(BF16)
    ua = jnp.dot(a, wua_ref[...], preferred_element_type=F32)
    ub = jnp.dot(b, wub_ref[...], preferred_element_type=F32)
    merged = (jax.nn.sigmoid(ga_ref[...].astype(F32)) * ua
              + jax.nn.sigmoid(gb_ref[...].astype(F32)) * ub)
    out = x_ref[...] + jnp.dot(merged.astype(BF16), wo_ref[...], preferred_element_type=F32)
    ms = jnp.mean(out * out, axis=-1, keepdims=True)
    y_ref[...] = out * lax.rsqrt(ms + EPS) * fnw_ref[...]


def _merge(x, o_a, o_b, rest, w_up_a, w_up_b, w_o, final_norm_w, tm):
    m = x.shape[0]
    const = dict(pipeline_mode=pl.Buffered(1))
    return pl.pallas_call(
        _merge_kernel,
        grid=(m // tm,),
        in_specs=[
            pl.BlockSpec((tm, D_MODEL), lambda i: (i, 0)),
            pl.BlockSpec((tm, FOX_WIDTH), lambda i: (i, 0)),
            pl.BlockSpec((tm, FOX_WIDTH), lambda i: (i, REST_ZA // FOX_WIDTH)),
            pl.BlockSpec((tm, RET_WIDTH), lambda i: (i, 0)),
            pl.BlockSpec((tm, RET_WIDTH), lambda i: (i, REST_ZB // RET_WIDTH)),
            pl.BlockSpec((tm, D_MODEL), lambda i: (i, REST_GA // D_MODEL)),
            pl.BlockSpec((tm, D_MODEL), lambda i: (i, REST_GB // D_MODEL)),
            pl.BlockSpec((FOX_WIDTH, D_MODEL), lambda i: (0, 0), **const),
            pl.BlockSpec((RET_WIDTH, D_MODEL), lambda i: (0, 0), **const),
            pl.BlockSpec((D_MODEL, D_MODEL), lambda i: (0, 0), **const),
            pl.BlockSpec((1, D_MODEL), lambda i: (0, 0)),
        ],
        out_specs=pl.BlockSpec((tm, D_MODEL), lambda i: (i, 0)),
        out_shape=jax.ShapeDtypeStruct((m, D_MODEL), F32),
        compiler_params=_cparams(("arbitrary",)),
        name="merge",
    )(x, o_a, rest, o_b, rest, rest, rest, w_up_a, w_up_b, w_o, final_norm_w)


def _rope_tables(pos):
    half = RET_HEAD_DIM // 2
    inv = 1.0 / (ROPE_BASE ** (jnp.arange(half, dtype=F32) / half))
    ang = pos.astype(F32)[:, None] * inv[None, :]
    return jnp.cos(ang), jnp.sin(ang)


def _layer(xp, xs, cache_k, cache_v, cache_logf, state_ret, page_table,
           norm_w, w_in, b_f, ret_norm_w, w_up_a, w_up_b, w_o, final_norm_w):
    batch, seq, _ = xp.shape
    nb = xs.shape[0]
    n_pool = cache_k.shape[0]
    past = page_table.shape[1] * PAGE_SIZE
    nh, hd, fw, rw = FOX_HEADS, FOX_HEAD_DIM, FOX_WIDTH, RET_WIDTH

    wt = w_in.T
    bf_pad = jnp.pad(b_f, (0, F_ROWS - nh)).reshape(1, F_ROWS)
    wua, wub, wo = w_up_a.astype(BF16), w_up_b.astype(BF16), w_o.astype(BF16)
    nw = norm_w.reshape(1, D_MODEL)
    rnw = ret_norm_w.reshape(1, rw)
    fnw = final_norm_w.reshape(1, D_MODEL)
    log_gamma = jnp.log(1.0 - 2.0 ** (-5.0 - jnp.arange(RET_HEADS, dtype=F32)))
    cos_p, sin_p = _rope_tables(jnp.arange(seq, dtype=jnp.int32))
    cos_s, sin_s = _rope_tables(past + jnp.arange(1, dtype=jnp.int32))

    x2 = xp.reshape(batch * seq, D_MODEL)
    k_p, v_p, rest_p, f_p = _inproj(x2, nw, wt, tm=1024, rest_dtype=BF16)
    logf_p, c_p = _logf_cumsum(f_p, bf_pad, batch, seq)
    o_a = _fox_prompt(rest_p, k_p, v_p, c_p, batch, seq)
    o_b, st_p = _ret_prompt(log_gamma, rest_p, cos_p, sin_p, rnw, batch, seq)
    y_p = _merge(x2, o_a, o_b, rest_p, wua, wub, wo, fnw, tm=256)

    xs2 = xs.reshape(nb, D_MODEL)
    k_s, v_s, rest_s, f_s = _inproj(xs2, nw, wt, tm=nb, rest_dtype=F32)
    logf_s = _logf(f_s, bf_pad)
    rest_s3 = rest_s.reshape(nb, 1, REST_WIDTH)
    q_s = rest_s[:, REST_QA:REST_QA + fw].reshape(nb, nh, hd)
    lfn = jnp.broadcast_to(logf_s[:, :, None], (nb, nh, hd))
    o_a_s = _fox_decode(
        page_table, q_s, k_s.reshape(nb, nh, hd), v_s.reshape(nb, nh, hd), lfn,
        cache_k.reshape(n_pool, PAIRS, 2, nh, hd), cache_v.reshape(n_pool, PAIRS, 2, nh, hd),
        jnp.swapaxes(cache_logf, 1, 2))
    o_b_s, st_s = _ret_sample(log_gamma, rest_s3, cos_s, sin_s, rnw, state_ret)
    y_s = _merge(xs2, o_a_s.reshape(nb, fw), o_b_s.reshape(nb, rw), rest_s,
                 wua, wub, wo, fnw, tm=nb)

    hshape = (nh, hd)
    return (y_p.reshape(batch, seq, D_MODEL), y_s.reshape(nb, 1, D_MODEL),
            k_p.reshape(batch, seq, *hshape), v_p.reshape(batch, seq, *hshape),
            logf_p.reshape(batch, seq, nh), st_p,
            k_s.reshape(nb, 1, *hshape), v_s.reshape(nb, 1, *hshape),
            logf_s.reshape(nb, 1, nh), st_s)


def kernel(x_prompt, x_sample, cache_k, cache_v, cache_logf, state_ret, page_table,
           norm_w, w_in, b_f, ret_norm_w, w_up_a, w_up_b, w_o, final_norm_w):
    depth = w_in.shape[0]
    assert depth == 1 and x_sample.shape[1] == 1
    outs = _layer(x_prompt, x_sample, cache_k[0], cache_v[0], cache_logf[0], state_ret[0],
                  page_table, norm_w[0], w_in[0], b_f[0], ret_norm_w[0], w_up_a[0],
                  w_up_b[0], w_o[0], final_norm_w)
    y_p, y_s, k_p, v_p, lf_p, st_p, k_s, v_s, lf_s, st_s = outs
    stack = lambda a: a[None]
    return (y_p, y_s, stack(k_p), stack(v_p), stack(lf_p), stack(st_p),
            stack(k_s), stack(v_s), stack(lf_s), stack(st_s))
```
